```python
import math
import jax, jax.numpy as jnp
from jax import lax
import numpy as np

D_MODEL = 2048
BATCH = 1
SEQ = 16384
DEPTH = 4
DEC_BATCH = 32
DEC_SEQ = 16
PAST_LEN = 4096

CHUNK = 64
N_EVEN = (DEPTH + 1) // 2
N_ODD = DEPTH // 2
ALPHA = (2 * DEPTH) ** 0.25
BETA = (8 * DEPTH) ** -0.25
EPS = 1e-5
NEG = -1e30

A_WIDTH = D_MODEL // 2
A_HEAD_DIM = 128
A_HEADS = A_WIDTH // A_HEAD_DIM
A_BAND = 8
A_WINDOW = A_BAND * CHUNK
REL_CLIP = 128
B_WIDTH = D_MODEL // 2
B_CONV = 3
C_WIDTH = D_MODEL // 2
C_HEAD_DIM = 128
C_HEADS = C_WIDTH // C_HEAD_DIM
D_WIDTH = D_MODEL // 2
D_HEAD_DIM = 64
D_HEADS = D_WIDTH // D_HEAD_DIM
D_GROUPS = 2
D_HG = D_HEADS // D_GROUPS
D_STATE = 128
D_CONV = 4
D_XBC = D_WIDTH + 2 * D_GROUPS * D_STATE
MEM_LEN = 256
M_WIDTH = D_MODEL // 4
M_HEADS = 4
M_HEAD_DIM = M_WIDTH // M_HEADS

MIX_WIDTH = A_WIDTH + B_WIDTH + M_WIDTH
EVEN_SPLITS = [A_WIDTH, A_WIDTH, A_WIDTH, B_WIDTH, B_WIDTH, B_WIDTH, M_WIDTH, MIX_WIDTH]
ODD_SPLITS = [C_WIDTH, C_WIDTH, C_WIDTH, C_WIDTH, C_HEADS, C_HEADS, D_XBC, D_HEADS, M_WIDTH, MIX_WIDTH]
EVEN_IN = sum(EVEN_SPLITS)
ODD_IN = sum(ODD_SPLITS)

kernel_name = "hybrid_streaming_encoder_step"


def split_last(x, sizes):
    offs = np.cumsum(sizes)[:-1].tolist()
    return jnp.split(x, offs, axis=-1)


def to_blocks(t, blk):
    b, l = t.shape[0], t.shape[1]
    return t.reshape(b, l // blk, blk, *t.shape[2:]).swapaxes(0, 1)


def from_blocks(t):
    t = t.swapaxes(0, 1)
    return t.reshape(t.shape[0], t.shape[1] * t.shape[2], *t.shape[3:])


def layer_norm(x, g, b):
    xf = x.astype(jnp.float32)
    mu = xf.mean(-1, keepdims=True)
    var = jnp.mean(jnp.square(xf - mu), -1, keepdims=True)
    return ((xf - mu) * lax.rsqrt(var + EPS) * g + b).astype(x.dtype)


def rms_norm(x, w):
    xf = x.astype(jnp.float32)
    return xf * lax.rsqrt(jnp.mean(xf * xf, -1, keepdims=True) + EPS) * w


def causal_dwconv(x, buf, w, b):
    width, length = w.shape[0], x.shape[1]
    xp = jnp.concatenate([buf.astype(x.dtype), x], axis=1)
    y = b + sum(xp[:, j:j + length] * w[j] for j in range(width))
    return y, xp[:, length:]


def rel_pos_bias(table, q_len, k_len, offset):
    dist = jnp.arange(q_len)[:, None] + offset - jnp.arange(k_len)[None, :]
    return table[:, jnp.clip(dist, -REL_CLIP, REL_CLIP) + REL_CLIP]


def band_attention_prompt(q, k, v, table):
    bn, s, h, d = q.shape
    nc = s // CHUNK
    qc = q.reshape(bn, nc, CHUNK, h, d)
    pad = ((0, 0), (A_BAND, 0), (0, 0), (0, 0), (0, 0))
    kc = jnp.pad(k.reshape(bn, nc, CHUNK, h, d), pad)
    vc = jnp.pad(v.reshape(bn, nc, CHUNK, h, d), pad)
    idx = jnp.arange(nc)[:, None] + jnp.arange(A_BAND + 1)[None, :]
    band = (A_BAND + 1) * CHUNK
    kb = kc[:, idx].reshape(bn, nc, band, h, d)
    vb = vc[:, idx].reshape(bn, nc, band, h, d)
    valid = jnp.repeat(idx >= A_BAND, CHUNK, axis=1)
    sc = jnp.einsum('bcqhd,bckhd->bchqk', qc, kb).astype(jnp.float32) * d ** -0.5
    sc = sc + rel_pos_bias(table, CHUNK, band, A_WINDOW).astype(jnp.float32)[None, None]
    sc = jnp.where(valid[None, :, None, None, :], sc, NEG)
    p = jax.nn.softmax(sc, axis=-1).astype(v.dtype)
    o = jnp.einsum('bchqk,bckhd->bcqhd', p, vb)
    return o.reshape(bn, s, h * d)


def band_attention_sample(q, k_all, v_all, table):
    bn, t, h, d = q.shape
    kl = k_all.shape[1]
    sc = jnp.einsum('bqhd,bkhd->bhqk', q, k_all).astype(jnp.float32) * d ** -0.5
    sc = sc + rel_pos_bias(table, t, kl, kl - t).astype(jnp.float32)[None]
    p = jax.nn.softmax(sc, axis=-1).astype(v_all.dtype)
    return jnp.einsum('bhqk,bkhd->bqhd', p, v_all).reshape(bn, t, h * d)


def mem_attention(q, mk, mv):
    bn, l, h, d = q.shape
    sc = jnp.einsum('blhd,bmhd->bhlm', q, mk.astype(q.dtype)).astype(jnp.float32) * d ** -0.5
    p = jax.nn.softmax(sc, axis=-1).astype(q.dtype)
    return jnp.einsum('bhlm,bmhd->blhd', p, mv.astype(q.dtype)).reshape(bn, l, h * d)


def mlstm_block(carry, inp):
    cs, ns, ms = carry
    q, k, v, ig, lf = inp
    t = q.shape[1]
    b = jnp.cumsum(lf, axis=1)
    bt = b[:, -1]
    causal = jnp.tril(jnp.ones((t, t), bool))
    dlog = b[:, :, None, :] - b[:, None, :, :] + ig[:, None, :, :]
    dlog = jnp.where(causal[None, :, :, None], dlog, -jnp.inf)
    g = b + ms[:, None, :]
    m = jnp.maximum(g, dlog.max(axis=2))
    w = jnp.einsum('bthd,bjhd->btjh', q, k) * jnp.exp(dlog - m[:, :, None, :])
    gw = jnp.exp(g - m)
    num = jnp.einsum('btjh,bjhd->bthd', w, v) + gw[..., None] * jnp.einsum('bthd,bhde->bthe', q, cs)
    nq = w.sum(axis=2) + gw * jnp.einsum('bthd,bhd->bth', q, ns)
    h = num / jnp.maximum(jnp.abs(nq), jnp.exp(-m))[..., None]
    wlog = bt[:, None, :] - b + ig
    m_new = jnp.maximum(bt + ms, wlog.max(axis=1))
    wj = jnp.exp(wlog - m_new[:, None, :])
    decay = jnp.exp(bt + ms - m_new)
    cs_new = decay[..., None, None] * cs + jnp.einsum('bjh,bjhd,bjhe->bhde', wj, k, v)
    ns_new = decay[..., None] * ns + jnp.einsum('bjh,bjhd->bhd', wj, k)
    return (cs_new, ns_new, m_new), h


def ssd_block(s, inp, a_neg):
    x, bm, cm, dt = inp
    t = x.shape[1]
    a = jnp.cumsum(dt * a_neg, axis=1)
    at = a[:, -1]
    causal = jnp.tril(jnp.ones((t, t), bool))
    llog = a[:, :, None] - a[:, None, :]
    lw = jnp.exp(jnp.where(causal[None, :, :, None, None], llog, -jnp.inf))
    cb = jnp.einsum('btgn,bjgn->btjg', cm, bm)
    y = jnp.einsum('btjgh,bjgh,bjghp->btghp', cb[..., None] * lw, dt, x)
    y = y + jnp.exp(a)[..., None] * jnp.einsum('btgn,bghpn->btghp', cm, s)
    wj = jnp.exp(at[:, None] - a) * dt
    s_new = jnp.exp(at)[..., None, None] * s + jnp.einsum('bjgh,bjghp,bjgn->bghpn', wj, x, bm)
    return s_new, y


def even_mixer(x, mk, mv, w_in, b_in, table, cw, cb, cache_k, cache_v, conv_buf):
    bn, l, _ = x.shape
    qa, ka, va, bx, bgate, cgate, qm, gate = split_last(x @ w_in + b_in, EVEN_SPLITS)
    qa, ka, va = (t.reshape(bn, l, A_HEADS, A_HEAD_DIM) for t in (qa, ka, va))
    if cache_k is None:
        ya = band_attention_prompt(qa, ka, va, table)
        new_k, new_v = ka[:, -A_WINDOW:], va[:, -A_WINDOW:]
    else:
        k_all = jnp.concatenate([cache_k.astype(ka.dtype), ka], axis=1)
        v_all = jnp.concatenate([cache_v.astype(va.dtype), va], axis=1)
        ya = band_attention_sample(qa, k_all, v_all, table)
        new_k, new_v = ka, va
    conv, new_buf = causal_dwconv(cgate * bx, conv_buf, cw, cb)
    yb = bgate * conv
    ym = mem_attention(qm.reshape(bn, l, M_HEADS, M_HEAD_DIM), mk, mv)
    mix = jnp.concatenate([ya, yb, ym], axis=-1) * jax.nn.silu(gate)
    return mix, (new_k, new_v, new_buf)


def odd_mixer(x, mk, mv, w_in, b_in, cw, cb, a_log, d_skip, norm_w, c_c, c_n, c_m, d_s, d_buf):
    f32 = jnp.float32
    bn, l, _ = x.shape
    blk = min(CHUNK, l)
    qc, kc, vc, oc, ic, fc, xbc, dt, qm, gate = split_last(x @ w_in + b_in, ODD_SPLITS)
    q = qc.reshape(bn, l, C_HEADS, C_HEAD_DIM).astype(f32)
    k = kc.reshape(bn, l, C_HEADS, C_HEAD_DIM).astype(f32) * C_HEAD_DIM ** -0.5
    v = vc.reshape(bn, l, C_HEADS, C_HEAD_DIM).astype(f32)
    ig = ic.astype(f32)
    lf = jax.nn.log_sigmoid(fc.astype(f32))
    init_c = (c_c.astype(f32), c_n.astype(f32), c_m.astype(f32))
    (cc_new, cn_new, cm_new), hc = lax.scan(mlstm_block, init_c, tuple(to_blocks(t, blk) for t in (q, k, v, ig, lf)))
    yc = from_blocks(hc).reshape(bn, l, C_WIDTH).astype(x.dtype) * jax.nn.sigmoid(oc)
    xbc_c, new_dbuf = causal_dwconv(xbc, d_buf, cw, cb)
    xd, bm, cm = split_last(jax.nn.silu(xbc_c), [D_WIDTH, D_GROUPS * D_STATE, D_GROUPS * D_STATE])
    xd = xd.reshape(bn, l, D_GROUPS, D_HG, D_HEAD_DIM).astype(f32)
    bm = bm.reshape(bn, l, D_GROUPS, D_STATE).astype(f32)
    cm = cm.reshape(bn, l, D_GROUPS, D_STATE).astype(f32)
    dtv = jax.nn.softplus(dt.astype(f32)).reshape(bn, l, D_GROUPS, D_HG)
    a_neg = -jnp.exp(a_log.astype(f32)).reshape(D_GROUPS, D_HG)
    s0 = d_s.astype(f32).reshape(bn, D_GROUPS, D_HG, D_HEAD_DIM, D_STATE)
    s_new, yd = lax.scan(lambda s, i: ssd_block(s, i, a_neg), s0, tuple(to_blocks(t, blk) for t in (xd, bm, cm, dtv)))
    yd = from_blocks(yd) + d_skip.astype(f32).reshape(D_GROUPS, D_HG)[:, :, None] * xd
    g_c, g_d, g_m = split_last(gate, [C_WIDTH, D_WIDTH, M_WIDTH])
    yd = rms_norm(yd.reshape(bn, l, D_WIDTH) * jax.nn.silu(g_d.astype(f32)), norm_w).astype(x.dtype)
    ym = mem_attention(qm.reshape(bn, l, M_HEADS, M_HEAD_DIM), mk, mv)
    mix = jnp.concatenate([yc * jax.nn.silu(g_c), yd, ym * jax.nn.silu(g_m)], axis=-1)
    return mix, (cc_new, cn_new, cm_new, s_new.reshape(bn, D_HEADS, D_HEAD_DIM, D_STATE), new_dbuf)


def setup_inputs(seed: int = 0) -> dict:
    key = jax.random.key(seed)
    ks = iter(jax.random.split(key, 48))
    nrm = lambda shape, scale=1.0: scale * jax.random.normal(next(ks), shape, jnp.float32)
    a_rows = min(A_WINDOW, PAST_LEN)
    odd_offs = np.cumsum([0] + ODD_SPLITS).tolist()
    f0, f1 = odd_offs[5], odd_offs[6]
    t0, t1 = odd_offs[7], odd_offs[8]
    b_in_odd = nrm((N_ODD, ODD_IN), 0.02)
    b_in_odd = b_in_odd.at[:, f0:f1].add(jnp.linspace(3.0, 6.0, C_HEADS))
    dt0 = jnp.exp(jax.random.uniform(next(ks), (N_ODD, D_HEADS), jnp.float32, math.log(1e-3), math.log(1e-1)))
    b_in_odd = b_in_odd.at[:, t0:t1].set(dt0 + jnp.log(-jnp.expm1(-dt0)))
    return {
        "x_prompt": nrm((BATCH, SEQ, D_MODEL)),
        "x_sample": nrm((DEC_BATCH, DEC_SEQ, D_MODEL)),
        "mem_prompt": nrm((BATCH, MEM_LEN, D_MODEL)),
        "cache_a_k": nrm((N_EVEN, DEC_BATCH, a_rows, A_HEADS, A_HEAD_DIM)),
        "cache_a_v": nrm((N_EVEN, DEC_BATCH, a_rows, A_HEADS, A_HEAD_DIM)),
        "cache_mem_k": nrm((DEPTH, DEC_BATCH, MEM_LEN, M_HEADS, M_HEAD_DIM)),
        "cache_mem_v": nrm((DEPTH, DEC_BATCH, MEM_LEN, M_HEADS, M_HEAD_DIM)),
        "state_b_conv": nrm((N_EVEN, DEC_BATCH, B_CONV - 1, B_WIDTH)),
        "state_c_C": nrm((N_ODD, DEC_BATCH, C_HEADS, C_HEAD_DIM, C_HEAD_DIM), 0.5),
        "state_c_n": nrm((N_ODD, DEC_BATCH, C_HEADS, C_HEAD_DIM), 0.5),
        "state_c_m": nrm((N_ODD, DEC_BATCH, C_HEADS), 0.5),
        "state_d_ssm": nrm((N_ODD, DEC_BATCH, D_HEADS, D_HEAD_DIM, D_STATE), 0.1),
        "state_d_conv": nrm((N_ODD, DEC_BATCH, D_CONV - 1, D_XBC)),
        "w_in_even": nrm((N_EVEN, D_MODEL, EVEN_IN), D_MODEL ** -0.5),
        "b_in_even": nrm((N_EVEN, EVEN_IN), 0.02),
        "w_in_odd": nrm((N_ODD, D_MODEL, ODD_IN), D_MODEL ** -0.5),
        "b_in_odd": b_in_odd,
        "w_out": nrm((DEPTH, MIX_WIDTH, D_MODEL), BETA * MIX_WIDTH ** -0.5),
        "w_mem_kv": nrm((DEPTH, D_MODEL, 2 * M_WIDTH), D_MODEL ** -0.5),
        "ln_g": 1.0 + nrm((DEPTH, D_MODEL), 0.02),
        "ln_b": nrm((DEPTH, D_MODEL), 0.02),
        "rel_bias_table": nrm((N_EVEN, A_HEADS, 2 * REL_CLIP + 1), 0.1),
        "b_conv_w": nrm((N_EVEN, B_CONV, B_WIDTH), B_CONV ** -0.5),
        "b_conv_b": nrm((N_EVEN, B_WIDTH), 0.02),
        "d_conv_w": nrm((N_ODD, D_CONV, D_XBC), D_CONV ** -0.5),
        "d_conv_b": nrm((N_ODD, D_XBC), 0.02),
        "d_A_log": jnp.log(jax.random.uniform(next(ks), (N_ODD, D_HEADS), jnp.float32, 1.0, 16.0)),
        "d_skip": 1.0 + nrm((N_ODD, D_HEADS), 0.02),
        "d_norm_w": 1.0 + nrm((N_ODD, D_WIDTH), 0.02),
    }


def reference(x_prompt, x_sample, mem_prompt, cache_a_k, cache_a_v, cache_mem_k, cache_mem_v,
              state_b_conv, state_c_C, state_c_n, state_c_m, state_d_ssm, state_d_conv,
              w_in_even, b_in_even, w_in_odd, b_in_odd, w_out, w_mem_kv, ln_g, ln_b,
              rel_bias_table, b_conv_w, b_conv_b, d_conv_w, d_conv_b, d_A_log, d_skip, d_norm_w):
    f32 = jnp.float32
    xp, xs = x_prompt, x_sample
    bp = xp.shape[0]
    p_mk, p_mv, p_ak, p_av, p_bc = [], [], [], [], []
    p_cc, p_cn, p_cm, p_ds, p_dc = [], [], [], [], []
    s_ak, s_av, s_bc = [], [], []
    s_cc, s_cn, s_cm, s_ds, s_dc = [], [], [], [], []
    for l in range(DEPTH):
        i = l // 2
        mkv = (mem_prompt @ w_mem_kv[l]).reshape(bp, MEM_LEN, 2, M_HEADS, M_HEAD_DIM)
        mk, mv = mkv[:, :, 0], mkv[:, :, 1]
        p_mk.append(mk)
        p_mv.append(mv)
        if l % 2 == 0:
            mix_p, (ak, av, bc) = even_mixer(xp, mk, mv, w_in_even[i], b_in_even[i], rel_bias_table[i],
                                             b_conv_w[i], b_conv_b[i], None, None,
                                             jnp.zeros((bp, B_CONV - 1, B_WIDTH), xp.dtype))
            mix_s, (sk, sv, sb) = even_mixer(xs, cache_mem_k[l], cache_mem_v[l], w_in_even[i], b_in_even[i],
                                             rel_bias_table[i], b_conv_w[i], b_conv_b[i],
                                             cache_a_k[i], cache_a_v[i], state_b_conv[i])
            p_ak.append(ak); p_av.append(av); p_bc.append(bc)
            s_ak.append(sk); s_av.append(sv); s_bc.append(sb)
        else:
            mix_p, (cc, cn, cm, ds, dc) = odd_mixer(
                xp, mk, mv, w_in_odd[i], b_in_odd[i], d_conv_w[i], d_conv_b[i], d_A_log[i], d_skip[i], d_norm_w[i],
                jnp.zeros((bp, C_HEADS, C_HEAD_DIM, C_HEAD_DIM), f32), jnp.zeros((bp, C_HEADS, C_HEAD_DIM), f32),
                jnp.zeros((bp, C_HEADS), f32), jnp.zeros((bp, D_HEADS, D_HEAD_DIM, D_STATE), f32),
                jnp.zeros((bp, D_CONV - 1, D_XBC), xp.dtype))
            mix_s, (scc, scn, scm, sds, sdc) = odd_mixer(
                xs, cache_mem_k[l], cache_mem_v[l], w_in_odd[i], b_in_odd[i], d_conv_w[i], d_conv_b[i], d_A_log[i],
                d_skip[i], d_norm_w[i], state_c_C[i], state_c_n[i], state_c_m[i], state_d_ssm[i], state_d_conv[i])
            p_cc.append(cc); p_cn.append(cn); p_cm.append(cm); p_ds.append(ds); p_dc.append(dc)
            s_cc.append(scc); s_cn.append(scn); s_cm.append(scm); s_ds.append(sds); s_dc.append(sdc)
        xp = layer_norm(ALPHA * xp + mix_p @ w_out[l], ln_g[l], ln_b[l])
        xs = layer_norm(ALPHA * xs + mix_s @ w_out[l], ln_g[l], ln_b[l])
    st = jnp.stack
    return (xp, xs,
            st(p_mk), st(p_mv), st(p_ak), st(p_av), st(p_bc),
            st(p_cc), st(p_cn), st(p_cm), st(p_ds), st(p_dc),
            st(s_ak), st(s_av), st(s_bc),
            st(s_cc), st(s_cn), st(s_cm), st(s_ds), st(s_dc))
```

```python
import functools

import jax
import jax.numpy as jnp
from jax import lax
from jax.experimental import pallas as pl
from jax.experimental.pallas import tpu as pltpu

F32 = jnp.float32
BF16 = jnp.bfloat16

D_MODEL = 2048
DEPTH = 4
CHUNK = 64
ALPHA = (2 * DEPTH) ** 0.25
EPS = 1e-5
NEG = -1e30

A_WIDTH = D_MODEL // 2
A_HEAD_DIM = 128
A_HEADS = A_WIDTH // A_HEAD_DIM
A_BAND = 8
A_WINDOW = A_BAND * CHUNK
REL_CLIP = 128
B_WIDTH = D_MODEL // 2
B_CONV = 3
C_WIDTH = D_MODEL // 2
C_HEAD_DIM = 128
C_HEADS = C_WIDTH // C_HEAD_DIM
D_WIDTH = D_MODEL // 2
D_HEAD_DIM = 64
D_HEADS = D_WIDTH // D_HEAD_DIM
D_GROUPS = 2
D_HG = D_HEADS // D_GROUPS
D_STATE = 128
D_CONV = 4
D_BC = 2 * D_GROUPS * D_STATE
D_XBC = D_WIDTH + D_BC
MEM_LEN = 256
M_WIDTH = D_MODEL // 4
M_HEADS = 4
M_HEAD_DIM = M_WIDTH // M_HEADS
MIX_WIDTH = A_WIDTH + B_WIDTH + M_WIDTH

LANES = 128
SUBLANES = 8
VMEM_LIMIT_BYTES = 48 * 1024 * 1024

EVEN_QA, EVEN_KA, EVEN_VA, EVEN_BX, EVEN_BGATE, EVEN_CGATE = 3, 4, 5, 6, 7, 8
ODD_QC, ODD_KC, ODD_VC, ODD_OC, ODD_XD = 3, 4, 5, 6, 7
QM_BLOCK = MIX_WIDTH // M_WIDTH
ODD_BC_BLOCK = (MIX_WIDTH + M_WIDTH + 5 * 1024) // D_BC
GATE_COLS = LANES
IG_OFF, FG_OFF, DT_OFF = 0, C_HEADS, 2 * C_HEADS


def _params(semantics):
    return pltpu.CompilerParams(dimension_semantics=semantics, vmem_limit_bytes=VMEM_LIMIT_BYTES)


def _nt_dot(a, b):
    return lax.dot_general(a, b, (((1,), (1,)), ((), ())), preferred_element_type=F32)


def _tn_dot(a, b):
    return lax.dot_general(a, b, (((0,), (0,)), ((), ())), preferred_element_type=F32)


def _silu(x):
    return x * jax.nn.sigmoid(x)


def _softplus(x):
    return jnp.maximum(x, 0.0) + jnp.log1p(jnp.exp(-jnp.abs(x)))


def _log_sigmoid(x):
    return -_softplus(-x)


def _linear_body(x_ref, w_ref, b_ref, o_ref, xb_ref):
    @pl.when(pl.program_id(1) == 0)
    def _():
        xb_ref[...] = x_ref[...].astype(BF16)

    o_ref[...] = jnp.dot(xb_ref[...], w_ref[...], preferred_element_type=F32) + b_ref[...]


def linear(x, w, b, *, tm, tn):
    m, k = x.shape
    n = w.shape[1]
    assert m % tm == 0 and n % tn == 0, (m, n, tm, tn)
    return pl.pallas_call(
        _linear_body,
        out_shape=jax.ShapeDtypeStruct((m, n), F32),
        grid=(m // tm, n // tn),
        in_specs=[
            pl.BlockSpec((tm, k), lambda i, j: (i, 0)),
            pl.BlockSpec((k, tn), lambda i, j: (0, j)),
            pl.BlockSpec((1, tn), lambda i, j: (0, j)),
        ],
        out_specs=pl.BlockSpec((tm, tn), lambda i, j: (i, j)),
        scratch_shapes=[pltpu.VMEM((tm, k), BF16)],
        compiler_params=_params(("parallel", "arbitrary")),
        name="linear",
    )(x, w, b)


def _relbias_body(tab_ref, o_ref, *, offset):
    h = pl.program_id(0)
    ql, kl = o_ref.shape[1], o_ref.shape[2]
    qi = lax.broadcasted_iota(jnp.int32, (ql, kl), 0)
    ki = lax.broadcasted_iota(jnp.int32, (ql, kl), 1)
    idx = jnp.clip(qi + offset - ki, -REL_CLIP, REL_CLIP) + REL_CLIP

    def step(t, acc):
        return jnp.where(idx == t, tab_ref[h, t], acc)

    o_ref[0] = lax.fori_loop(0, 2 * REL_CLIP + 1, step, jnp.zeros((ql, kl), F32))


def rel_bias(table, q_len, k_len, offset):
    heads = table.shape[0]
    return pl.pallas_call(
        functools.partial(_relbias_body, offset=offset),
        out_shape=jax.ShapeDtypeStruct((heads, q_len, k_len), F32),
        grid=(heads,),
        in_specs=[pl.BlockSpec(memory_space=pltpu.SMEM)],
        out_specs=pl.BlockSpec((1, q_len, k_len), lambda h: (h, 0, 0)),
        compiler_params=_params(("arbitrary",)),
        name="rel_bias",
    )(table)


BAND_ROWS = (A_BAND + 1) * CHUNK


def _band_prompt_body(q_ref, k_ref, v_ref, bias_ref, o_ref, kwin, vwin):
    i = pl.program_id(0)
    tq = q_ref.shape[0]
    scale = A_HEAD_DIM ** -0.5

    @pl.when(i == 0)
    def _():
        kwin[0:tq] = jnp.zeros((tq, A_WIDTH), BF16)
        vwin[0:tq] = jnp.zeros((tq, A_WIDTH), BF16)

    @pl.when(i > 0)
    def _():
        kwin[0:tq] = kwin[tq:2 * tq]
        vwin[0:tq] = vwin[tq:2 * tq]

    kwin[tq:2 * tq] = k_ref[...].astype(BF16)
    vwin[tq:2 * tq] = v_ref[...].astype(BF16)

    kidx = lax.broadcasted_iota(jnp.int32, (1, BAND_ROWS), 1)

    def chunk_step(c, carry):
        r0 = pl.multiple_of(c * CHUNK, CHUNK)
        valid = kidx + r0 >= jnp.where(i > 0, 0, tq)
        for h in range(A_HEADS):
            hs = slice(h * A_HEAD_DIM, (h + 1) * A_HEAD_DIM)
            qh = q_ref[pl.ds(r0, CHUNK), hs].astype(BF16)
            kb = kwin[pl.ds(r0, BAND_ROWS), hs]
            vb = vwin[pl.ds(r0, BAND_ROWS), hs]
            s = _nt_dot(qh, kb) * scale + bias_ref[h]
            s = jnp.where(valid, s, NEG)
            m = jnp.max(s, axis=-1, keepdims=True)
            p = jnp.exp(s - m)
            inv = 1.0 / jnp.sum(p, axis=-1, keepdims=True)
            o = jnp.dot(p.astype(BF16), vb, preferred_element_type=F32)
            o_ref[pl.ds(r0, CHUNK), hs] = o * inv
        return carry

    lax.fori_loop(0, tq // CHUNK, chunk_step, 0)


def band_attention_prompt(proj, bias, *, tq):
    s = proj.shape[0]
    assert tq >= A_WINDOW and tq % CHUNK == 0 and s % tq == 0
    return pl.pallas_call(
        _band_prompt_body,
        out_shape=jax.ShapeDtypeStruct((s, A_WIDTH), F32),
        grid=(s // tq,),
        in_specs=[
            pl.BlockSpec((tq, A_WIDTH), lambda i: (i, EVEN_QA)),
            pl.BlockSpec((tq, A_WIDTH), lambda i: (i, EVEN_KA)),
            pl.BlockSpec((tq, A_WIDTH), lambda i: (i, EVEN_VA)),
            pl.BlockSpec((A_HEADS, CHUNK, BAND_ROWS), lambda i: (0, 0, 0)),
        ],
        out_specs=pl.BlockSpec((tq, A_WIDTH), lambda i: (i, 0)),
        scratch_shapes=[pltpu.VMEM((2 * tq, A_WIDTH), BF16), pltpu.VMEM((2 * tq, A_WIDTH), BF16)],
        compiler_params=_params(("arbitrary",)),
        name="band_attention_prompt",
    )(proj, proj, proj, bias)


def _band_sample_body(q_ref, nk_ref, nv_ref, ck_ref, cv_ref, bc_ref, bn_ref, o_ref):
    scale = A_HEAD_DIM ** -0.5
    for h in range(A_HEADS):
        hs = slice(h * A_HEAD_DIM, (h + 1) * A_HEAD_DIM)
        qh = q_ref[:, hs].astype(BF16)
        s1 = _nt_dot(qh, ck_ref[0, :, hs].astype(BF16)) * scale + bc_ref[h]
        s2 = _nt_dot(qh, nk_ref[:, hs].astype(BF16)) * scale + bn_ref[h]
        m = jnp.maximum(jnp.max(s1, axis=-1, keepdims=True), jnp.max(s2, axis=-1, keepdims=True))
        p1 = jnp.exp(s1 - m)
        p2 = jnp.exp(s2 - m)
        inv = 1.0 / (jnp.sum(p1, axis=-1, keepdims=True) + jnp.sum(p2, axis=-1, keepdims=True))
        o = jnp.dot(p1.astype(BF16), cv_ref[0, :, hs].astype(BF16), preferred_element_type=F32)
        o = o + jnp.dot(p2.astype(BF16), nv_ref[:, hs].astype(BF16), preferred_element_type=F32)
        o_ref[:, hs] = o * inv


def band_attention_sample(proj, cache_k, cache_v, bias_cache, bias_new, *, t):
    b, lc, _ = cache_k.shape
    return pl.pallas_call(
        _band_sample_body,
        out_shape=jax.ShapeDtypeStruct((b * t, A_WIDTH), F32),
        grid=(b,),
        in_specs=[
            pl.BlockSpec((t, A_WIDTH), lambda i: (i, EVEN_QA)),
            pl.BlockSpec((t, A_WIDTH), lambda i: (i, EVEN_KA)),
            pl.BlockSpec((t, A_WIDTH), lambda i: (i, EVEN_VA)),
            pl.BlockSpec((1, lc, A_WIDTH), lambda i: (i, 0, 0)),
            pl.BlockSpec((1, lc, A_WIDTH), lambda i: (i, 0, 0)),
            pl.BlockSpec((A_HEADS, t, lc), lambda i: (0, 0, 0)),
            pl.BlockSpec((A_HEADS, t, t), lambda i: (0, 0, 0)),
        ],
        out_specs=pl.BlockSpec((t, A_WIDTH), lambda i: (i, 0)),
        compiler_params=_params(("parallel",)),
        name="band_attention_sample",
    )(proj, proj, proj, cache_k, cache_v, bias_cache, bias_new)


def _mem_attn_body(q_ref, mk_ref, mv_ref, o_ref):
    scale = M_HEAD_DIM ** -0.5
    for h in range(M_HEADS):
        hs = slice(h * M_HEAD_DIM, (h + 1) * M_HEAD_DIM)
        qh = q_ref[:, hs].astype(BF16)
        s = _nt_dot(qh, mk_ref[0, :, hs].astype(BF16)) * scale
        m = jnp.max(s, axis=-1, keepdims=True)
        p = jnp.exp(s - m)
        inv = 1.0 / jnp.sum(p, axis=-1, keepdims=True)
        o = jnp.dot(p.astype(BF16), mv_ref[0, :, hs].astype(BF16), preferred_element_type=F32)
        o_ref[:, hs] = o * inv


def mem_attention(proj, mk, mv, *, seq, tl):
    b = mk.shape[0]
    nt = seq // tl
    return pl.pallas_call(
        _mem_attn_body,
        out_shape=jax.ShapeDtypeStruct((b * seq, M_WIDTH), F32),
        grid=(b, nt),
        in_specs=[
            pl.BlockSpec((tl, M_WIDTH), lambda i, j: (i * nt + j, QM_BLOCK)),
            pl.BlockSpec((1, MEM_LEN, M_WIDTH), lambda i, j: (i, 0, 0)),
            pl.BlockSpec((1, MEM_LEN, M_WIDTH), lambda i, j: (i, 0, 0)),
        ],
        out_specs=pl.BlockSpec((tl, M_WIDTH), lambda i, j: (i * nt + j, 0)),
        compiler_params=_params(("parallel", "parallel")),
        name="mem_attention",
    )(proj, mk, mv)


def _shift_rows(x, halo, s):
    rolled = pltpu.roll(x, s, 0)
    halo_rolled = pltpu.roll(halo, s, 0)
    row = lax.broadcasted_iota(jnp.int32, halo.shape, 0)
    top = jnp.where(row < s, halo_rolled, rolled[0:SUBLANES])
    if x.shape[0] == SUBLANES:
        return top
    return jnp.concatenate([top, rolled[SUBLANES:]], axis=0)


def _causal_dwconv(x, halo, w_ref, b_ref):
    width = w_ref.shape[0]
    y = b_ref[...]
    for j in range(width):
        s = width - 1 - j
        xs = x if s == 0 else _shift_rows(x, halo, s)
        y = y + xs * w_ref[j:j + 1, :]
    return y


def _even_mix_body(gate_ref, bx_ref, bg_ref, cg_ref, ya_ref, ym_ref, w_ref, b_ref, halo_ref,
                   mix_ref, tail_ref, carry):
    @pl.when(pl.program_id(1) == 0)
    def _():
        carry[...] = halo_ref[0]

    tm = bx_ref.shape[0]
    u = cg_ref[...] * bx_ref[...]
    conv = _causal_dwconv(u, carry[...], w_ref, b_ref)
    yb = bg_ref[...] * conv
    last = u[tm - SUBLANES:tm]
    carry[...] = last
    tail_ref[0] = last
    sg = _silu(gate_ref[...])
    mix_ref[:, 0:A_WIDTH] = ya_ref[...] * sg[:, 0:A_WIDTH]
    mix_ref[:, A_WIDTH:A_WIDTH + B_WIDTH] = yb * sg[:, A_WIDTH:A_WIDTH + B_WIDTH]
    mix_ref[:, A_WIDTH + B_WIDTH:MIX_WIDTH] = ym_ref[...] * sg[:, A_WIDTH + B_WIDTH:MIX_WIDTH]


def even_mix(proj, ya, ym, conv_w, conv_b, halo, *, seq, tm):
    b = halo.shape[0]
    nt = seq // tm
    row = lambda i, j: i * nt + j
    return pl.pallas_call(
        _even_mix_body,
        out_shape=(jax.ShapeDtypeStruct((b * seq, MIX_WIDTH), F32),
                   jax.ShapeDtypeStruct((b, SUBLANES, B_WIDTH), F32)),
        grid=(b, nt),
        in_specs=[
            pl.BlockSpec((tm, MIX_WIDTH), lambda i, j: (row(i, j), 0)),
            pl.BlockSpec((tm, B_WIDTH), lambda i, j: (row(i, j), EVEN_BX)),
            pl.BlockSpec((tm, B_WIDTH), lambda i, j: (row(i, j), EVEN_BGATE)),
            pl.BlockSpec((tm, B_WIDTH), lambda i, j: (row(i, j), EVEN_CGATE)),
            pl.BlockSpec((tm, A_WIDTH), lambda i, j: (row(i, j), 0)),
            pl.BlockSpec((tm, M_WIDTH), lambda i, j: (row(i, j), 0)),
            pl.BlockSpec((B_CONV, B_WIDTH), lambda i, j: (0, 0)),
            pl.BlockSpec((1, B_WIDTH), lambda i, j: (0, 0)),
            pl.BlockSpec((1, SUBLANES, B_WIDTH), lambda i, j: (i, 0, 0)),
        ],
        out_specs=(pl.BlockSpec((tm, MIX_WIDTH), lambda i, j: (row(i, j), 0)),
                   pl.BlockSpec((1, SUBLANES, B_WIDTH), lambda i, j: (i, 0, 0))),
        scratch_shapes=[pltpu.VMEM((SUBLANES, B_WIDTH), F32)],
        compiler_params=_params(("arbitrary", "arbitrary")),
        name="even_mix",
    )(proj, proj, proj, proj, ya, ym, conv_w, conv_b, halo)


def _odd_mix_body(gate_ref, oc_ref, hc_ref, yd_ref, ym_ref, nw_ref, mix_ref):
    sg = _silu(gate_ref[...])
    yc = hc_ref[...] * jax.nn.sigmoid(oc_ref[...])
    mix_ref[:, 0:C_WIDTH] = yc * sg[:, 0:C_WIDTH]
    z = yd_ref[...] * sg[:, C_WIDTH:C_WIDTH + D_WIDTH]
    zn = z * lax.rsqrt(jnp.mean(z * z, axis=-1, keepdims=True) + EPS)
    mix_ref[:, C_WIDTH:C_WIDTH + D_WIDTH] = zn * nw_ref[...]
    mix_ref[:, C_WIDTH + D_WIDTH:MIX_WIDTH] = ym_ref[...] * sg[:, C_WIDTH + D_WIDTH:MIX_WIDTH]


def odd_mix(proj, hc, yd, ym, norm_w, *, tm):
    m = proj.shape[0]
    return pl.pallas_call(
        _odd_mix_body,
        out_shape=jax.ShapeDtypeStruct((m, MIX_WIDTH), F32),
        grid=(m // tm,),
        in_specs=[
            pl.BlockSpec((tm, MIX_WIDTH), lambda i: (i, 0)),
            pl.BlockSpec((tm, C_WIDTH), lambda i: (i, ODD_OC)),
            pl.BlockSpec((tm, C_WIDTH), lambda i: (i, 0)),
            pl.BlockSpec((tm, D_WIDTH), lambda i: (i, 0)),
            pl.BlockSpec((tm, M_WIDTH), lambda i: (i, 0)),
            pl.BlockSpec((1, D_WIDTH), lambda i: (0, 0)),
        ],
        out_specs=pl.BlockSpec((tm, MIX_WIDTH), lambda i: (i, 0)),
        compiler_params=_params(("parallel",)),
        name="odd_mix",
    )(proj, proj, hc, yd, ym, norm_w)


def _outproj_ln_body(mix_ref, w_ref, x_ref, g_ref, b_ref, o_ref):
    acc = jnp.dot(mix_ref[...].astype(BF16), w_ref[...], preferred_element_type=F32)
    y = ALPHA * x_ref[...] + acc
    mu = jnp.mean(y, axis=-1, keepdims=True)
    d = y - mu
    var = jnp.mean(d * d, axis=-1, keepdims=True)
    o_ref[...] = d * lax.rsqrt(var + EPS) * g_ref[...] + b_ref[...]


def outproj_ln(mix, w, x, g, b, *, tm):
    m = x.shape[0]
    return pl.pallas_call(
        _outproj_ln_body,
        out_shape=jax.ShapeDtypeStruct((m, D_MODEL), F32),
        grid=(m // tm,),
        in_specs=[
            pl.BlockSpec((tm, MIX_WIDTH), lambda i: (i, 0)),
            pl.BlockSpec((MIX_WIDTH, D_MODEL), lambda i: (0, 0)),
            pl.BlockSpec((tm, D_MODEL), lambda i: (i, 0)),
            pl.BlockSpec((1, D_MODEL), lambda i: (0, 0)),
            pl.BlockSpec((1, D_MODEL), lambda i: (0, 0)),
        ],
        out_specs=pl.BlockSpec((tm, D_MODEL), lambda i: (i, 0)),
        compiler_params=_params(("parallel",)),
        name="outproj_ln",
    )(mix, w, x, g, b)


def _tri_masks(t):
    ri = lax.broadcasted_iota(jnp.int32, (t, t), 0)
    ci = lax.broadcasted_iota(jnp.int32, (t, t), 1)
    return ri >= ci, ri <= ci


def _cumsum_cols(lower, x):
    return jnp.dot(lower.astype(F32), x, precision=lax.Precision.HIGHEST, preferred_element_type=F32)


def _cumsum_rows(upper, x):
    return jnp.dot(x, upper.astype(F32), precision=lax.Precision.HIGHEST, preferred_element_type=F32)


def _mlstm_body(q_ref, k_ref, v_ref, gc_ref, gr_ref, c0_ref, n0_ref, m0_ref,
                h_ref, c_ref, n_ref, m_ref):
    @pl.when(pl.program_id(1) == 0)
    def _():
        c_ref[...] = c0_ref[...]
        n_ref[...] = n0_ref[...]
        m_ref[...] = m0_ref[...]

    t = q_ref.shape[0]
    scale = C_HEAD_DIM ** -0.5
    lower, upper = _tri_masks(t)
    gc = gc_ref[...]
    gr = gr_ref[0]
    ig_c = gc[:, IG_OFF:IG_OFF + C_HEADS]
    ig_r = gr[IG_OFF:IG_OFF + C_HEADS, :]
    b_c = _cumsum_cols(lower, _log_sigmoid(gc))[:, FG_OFF:FG_OFF + C_HEADS]
    b_r = _cumsum_rows(upper, _log_sigmoid(gr))[FG_OFF:FG_OFF + C_HEADS, :]
    for h in range(C_HEADS):
        hs = slice(h * C_HEAD_DIM, (h + 1) * C_HEAD_DIM)
        bc = b_c[:, h:h + 1]
        ms = m_ref[0, h:h + 1, 0:1]
        dlog = jnp.where(lower, bc - b_r[h:h + 1, :] + ig_r[h:h + 1, :], -jnp.inf)
        g = bc + ms
        m = jnp.maximum(g, jnp.max(dlog, axis=1, keepdims=True))
        qf = q_ref[:, hs]
        kf = k_ref[:, hs] * scale
        qb = qf.astype(BF16)
        kb = kf.astype(BF16)
        vb = v_ref[:, hs].astype(BF16)
        w = _nt_dot(qb, kb) * jnp.exp(dlog - m)
        gw = jnp.exp(g - m)
        cs = c_ref[0, h]
        ns = n_ref[0, h:h + 1, :]
        num = jnp.dot(w.astype(BF16), vb, preferred_element_type=F32)
        num = num + gw * jnp.dot(qb, cs.astype(BF16), preferred_element_type=F32)
        nq = jnp.sum(w, axis=1, keepdims=True) + gw * jnp.sum(qf * ns, axis=1, keepdims=True)
        h_ref[:, hs] = num * (1.0 / jnp.maximum(jnp.abs(nq), jnp.exp(-m)))
        bt = bc[t - 1:t, :]
        wlog = bt - bc + ig_c[:, h:h + 1]
        m_new = jnp.maximum(bt + ms, jnp.max(wlog, axis=0, keepdims=True))
        wj = jnp.exp(wlog - m_new)
        decay = jnp.exp(bt + ms - m_new)
        kw = kf * wj
        c_ref[0, h] = decay * cs + _tn_dot(kw.astype(BF16), vb)
        n_ref[0, h:h + 1, :] = decay * ns + jnp.sum(kw, axis=0, keepdims=True)
        m_ref[0, h:h + 1, :] = jnp.broadcast_to(m_new, (1, C_HEAD_DIM))


def mlstm(proj, gates_c, gates_r, c0, n0, m0, *, seq, t):
    b = c0.shape[0]
    nb = seq // t
    row = lambda i, j: i * nb + j
    state = lambda shape: pl.BlockSpec((1,) + shape, lambda i, j: (i,) + (0,) * len(shape))
    return pl.pallas_call(
        _mlstm_body,
        out_shape=(jax.ShapeDtypeStruct((b * seq, C_WIDTH), F32),
                   jax.ShapeDtypeStruct(c0.shape, F32),
                   jax.ShapeDtypeStruct(n0.shape, F32),
                   jax.ShapeDtypeStruct(m0.shape, F32)),
        grid=(b, nb),
        in_specs=[
            pl.BlockSpec((t, C_WIDTH), lambda i, j: (row(i, j), ODD_QC)),
            pl.BlockSpec((t, C_WIDTH), lambda i, j: (row(i, j), ODD_KC)),
            pl.BlockSpec((t, C_WIDTH), lambda i, j: (row(i, j), ODD_VC)),
            pl.BlockSpec((t, GATE_COLS), lambda i, j: (row(i, j), 0)),
            pl.BlockSpec((1,) + gates_r.shape[1:], lambda i, j: (row(i, j), 0, 0)),
            state(c0.shape[1:]), state(n0.shape[1:]), state(m0.shape[1:]),
        ],
        out_specs=(pl.BlockSpec((t, C_WIDTH), lambda i, j: (row(i, j), 0)),
                   state(c0.shape[1:]), state(n0.shape[1:]), state(m0.shape[1:])),
        compiler_params=_params(("arbitrary", "arbitrary")),
        name="mlstm",
    )(proj, proj, proj, gates_c, gates_r, c0, n0, m0)


def _ssd_body(xd_ref, bc_ref, gc_ref, gr_ref, wx_ref, wbc_ref, bx_ref, bbc_ref,
              alr_ref, alc_ref, dskip_ref, hx0_ref, hbc0_ref, s0_ref,
              y_ref, s_ref, hxo_ref, hbco_ref, hx, hbc):
    @pl.when(pl.program_id(1) == 0)
    def _():
        s_ref[...] = s0_ref[...]
        hx[...] = hx0_ref[0]
        hbc[...] = hbc0_ref[0]

    t = xd_ref.shape[0]
    xr = xd_ref[...]
    br = bc_ref[...]
    xs = _silu(_causal_dwconv(xr, hx[...], wx_ref, bx_ref))
    bcs = _silu(_causal_dwconv(br, hbc[...], wbc_ref, bbc_ref))
    hx[...] = xr[t - SUBLANES:t]
    hbc[...] = br[t - SUBLANES:t]
    hxo_ref[0] = xr[t - SUBLANES:t]
    hbco_ref[0] = br[t - SUBLANES:t]

    lower, upper = _tri_masks(t)
    dt_c = _softplus(gc_ref[:, DT_OFF:DT_OFF + D_HEADS])
    dt_r = _softplus(gr_ref[0, DT_OFF:DT_OFF + D_HEADS, :])
    a_c = _cumsum_cols(lower, dt_c * -jnp.exp(alr_ref[...]))
    a_r = _cumsum_rows(upper, dt_r * -jnp.exp(alc_ref[...]))
    for g in range(D_GROUPS):
        bmg = bcs[:, g * D_STATE:(g + 1) * D_STATE].astype(BF16)
        cmg = bcs[:, (D_GROUPS + g) * D_STATE:(D_GROUPS + g + 1) * D_STATE].astype(BF16)
        cb = _nt_dot(cmg, bmg)
        for hh in range(D_HG):
            h = g * D_HG + hh
            hs = slice(h * D_HEAD_DIM, (h + 1) * D_HEAD_DIM)
            ac = a_c[:, h:h + 1]
            dtc = dt_c[:, h:h + 1]
            lw = jnp.exp(jnp.where(lower, ac - a_r[h:h + 1, :], -jnp.inf))
            xh = xs[:, hs]
            sh = s_ref[0, h]
            y = jnp.dot((cb * lw).astype(BF16), (xh * dtc).astype(BF16), preferred_element_type=F32)
            y = y + jnp.exp(ac) * _nt_dot(cmg, sh.astype(BF16))
            y_ref[:, hs] = y + dskip_ref[:, hs] * xh
            at = ac[t - 1:t, :]
            wj = jnp.exp(at - ac) * dtc
            s_ref[0, h] = jnp.exp(at) * sh + _tn_dot((xh * wj).astype(BF16), bmg)


def ssd(proj, gates_c, gates_r, wx, wbc, bx, bbc, alog_r, alog_c, dskip, hx0, hbc0, s0, *, seq, t):
    b = s0.shape[0]
    nb = seq // t
    row = lambda i, j: i * nb + j
    full = lambda a: pl.BlockSpec(a.shape, lambda i, j: (0,) * a.ndim)
    state = lambda shape: pl.BlockSpec((1,) + shape, lambda i, j: (i,) + (0,) * len(shape))
    return pl.pallas_call(
        _ssd_body,
        out_shape=(jax.ShapeDtypeStruct((b * seq, D_WIDTH), F32),
                   jax.ShapeDtypeStruct(s0.shape, F32),
                   jax.ShapeDtypeStruct(hx0.shape, F32),
                   jax.ShapeDtypeStruct(hbc0.shape, F32)),
        grid=(b, nb),
        in_specs=[
            pl.BlockSpec((t, D_WIDTH), lambda i, j: (row(i, j), ODD_XD)),
            pl.BlockSpec((t, D_BC), lambda i, j: (row(i, j), ODD_BC_BLOCK)),
            pl.BlockSpec((t, GATE_COLS), lambda i, j: (row(i, j), 0)),
            pl.BlockSpec((1,) + gates_r.shape[1:], lambda i, j: (row(i, j), 0, 0)),
            full(wx), full(wbc), full(bx), full(bbc), full(alog_r), full(alog_c), full(dskip),
            state(hx0.shape[1:]), state(hbc0.shape[1:]), state(s0.shape[1:]),
        ],
        out_specs=(pl.BlockSpec((t, D_WIDTH), lambda i, j: (row(i, j), 0)),
                   state(s0.shape[1:]), state(hx0.shape[1:]), state(hbc0.shape[1:])),
        scratch_shapes=[pltpu.VMEM((SUBLANES, D_WIDTH), F32), pltpu.VMEM((SUBLANES, D_BC), F32)],
        compiler_params=_params(("arbitrary", "arbitrary")),
        name="ssd",
    )(proj, proj, gates_c, gates_r, wx, wbc, bx, bbc, alog_r, alog_c, dskip, hx0, hbc0, s0)


def _pad_state_rows(buf):
    b, r, c = buf.shape
    return jnp.concatenate([jnp.zeros((b, SUBLANES - r, c), F32), buf], axis=1)


def _row_gates(gates_c, t):
    m = gates_c.shape[0]
    used = 2 * C_HEADS + D_HEADS
    return gates_c[:, :used].reshape(m // t, t, used).transpose(0, 2, 1)


def _even_weights(w_in, b_in):
    off = A_WIDTH * 3 + B_WIDTH * 3
    order = lambda a: jnp.concatenate([a[..., off + M_WIDTH:], a[..., off:off + M_WIDTH], a[..., :off]], axis=-1)
    return order(w_in).astype(BF16), order(b_in)[None, :]


def _odd_weights(w_in, b_in):
    c4 = 4 * C_WIDTH
    g0 = c4 + 2 * C_HEADS
    d0 = g0 + D_XBC
    q0 = d0 + D_HEADS
    t0 = q0 + M_WIDTH
    big = lambda a: jnp.concatenate([a[..., t0:], a[..., q0:t0], a[..., :c4], a[..., g0:d0]], axis=-1)
    pad = GATE_COLS - 2 * C_HEADS - D_HEADS
    small = lambda a: jnp.concatenate(
        [a[..., c4:g0], a[..., d0:q0], jnp.zeros(a.shape[:-1] + (pad,), a.dtype)], axis=-1)
    return big(w_in).astype(BF16), big(b_in)[None, :], small(w_in).astype(BF16), small(b_in)[None, :]


def _tiles(m):
    return dict(tm=min(m, 1024), tn=512)


def _even_layer(x, proj_w, proj_b, mk, mv, table, conv_w, conv_b, halo, cache, *, seq):
    m = x.shape[0]
    b = m // seq
    proj = linear(x, proj_w, proj_b, **_tiles(m))
    ka = proj[:, EVEN_KA * A_WIDTH:(EVEN_KA + 1) * A_WIDTH]
    va = proj[:, EVEN_VA * A_WIDTH:(EVEN_VA + 1) * A_WIDTH]
    if cache is None:
        bias = rel_bias(table, CHUNK, BAND_ROWS, A_WINDOW)
        ya = band_attention_prompt(proj, bias, tq=A_WINDOW)
        new_k = ka.reshape(b, seq, A_HEADS, A_HEAD_DIM)[:, -A_WINDOW:]
        new_v = va.reshape(b, seq, A_HEADS, A_HEAD_DIM)[:, -A_WINDOW:]
    else:
        cache_k, cache_v = cache
        lc = cache_k.shape[1]
        bias = rel_bias(table, seq, lc + seq, lc)
        ya = band_attention_sample(proj, cache_k.reshape(b, lc, A_WIDTH), cache_v.reshape(b, lc, A_WIDTH),
                                   bias[:, :, :lc], bias[:, :, lc:], t=seq)
        new_k = ka.reshape(b, seq, A_HEADS, A_HEAD_DIM)
        new_v = va.reshape(b, seq, A_HEADS, A_HEAD_DIM)
    ym = mem_attention(proj, mk, mv, seq=seq, tl=min(seq, 512))
    mix, tail = even_mix(proj, ya, ym, conv_w, conv_b, halo, seq=seq, tm=min(seq, 256))
    return mix, new_k, new_v, tail[:, SUBLANES - (B_CONV - 1):]


def _odd_layer(x, big_w, big_b, small_w, small_b, mk, mv, conv_w, conv_b, a_log, d_skip, norm_w,
               c0, n0, m0, s0, dbuf, *, seq):
    m = x.shape[0]
    t = min(CHUNK, seq)
    proj = linear(x, big_w, big_b, **_tiles(m))
    gates_c = linear(x, small_w, small_b, tm=min(m, 1024), tn=GATE_COLS)
    gates_r = _row_gates(gates_c, t)
    m0_rep = jnp.broadcast_to(m0[:, :, None], m0.shape + (C_HEAD_DIM,))
    hc, cc, cn, cm = mlstm(proj, gates_c, gates_r, c0, n0, m0_rep, seq=seq, t=t)
    halo = _pad_state_rows(dbuf)
    yd, s_new, hx, hbc = ssd(
        proj, gates_c, gates_r,
        conv_w[:, :D_WIDTH], conv_w[:, D_WIDTH:], conv_b[None, :D_WIDTH], conv_b[None, D_WIDTH:],
        a_log[None, :], a_log[:, None], jnp.repeat(d_skip, D_HEAD_DIM)[None, :],
        halo[:, :, :D_WIDTH], halo[:, :, D_WIDTH:], s0, seq=seq, t=t)
    new_dbuf = jnp.concatenate([hx, hbc], axis=-1)[:, SUBLANES - (D_CONV - 1):]
    ym = mem_attention(proj, mk, mv, seq=seq, tl=min(seq, 512))
    mix = odd_mix(proj, hc, yd, ym, norm_w[None, :], tm=min(m, 256))
    return mix, cc, cn, cm[:, :, 0], s_new, new_dbuf


def kernel(x_prompt, x_sample, mem_prompt, cache_a_k, cache_a_v, cache_mem_k, cache_mem_v, state_b_conv, state_c_C, state_c_n, state_c_m, state_d_ssm, state_d_conv, w_in_even, b_in_even, w_in_odd, b_in_odd, w_out, w_mem_kv, ln_g, ln_b, rel_bias_table, b_conv_w, b_conv_b, d_conv_w, d_conv_b, d_A_log, d_skip, d_norm_w):
    bp, sp, _ = x_prompt.shape
    bs, ss, _ = x_sample.shape
    xp = x_prompt.reshape(bp * sp, D_MODEL)
    xs = x_sample.reshape(bs * ss, D_MODEL)
    mem = mem_prompt.reshape(bp * MEM_LEN, D_MODEL)
    outs = {k: [] for k in ("p_mk", "p_mv", "p_ak", "p_av", "p_bc", "p_cc", "p_cn", "p_cm", "p_ds", "p_dc",
                            "s_ak", "s_av", "s_bc", "s_cc", "s_cn", "s_cm", "s_ds", "s_dc")}
    for l in range(DEPTH):
        i = l // 2
        mkv = linear(mem, w_mem_kv[l].astype(BF16), jnp.zeros((1, 2 * M_WIDTH), F32), tm=MEM_LEN, tn=512)
        mk = mkv[:, :M_WIDTH].reshape(bp, MEM_LEN, M_WIDTH)
        mv = mkv[:, M_WIDTH:].reshape(bp, MEM_LEN, M_WIDTH)
        outs["p_mk"].append(mk.reshape(bp, MEM_LEN, M_HEADS, M_HEAD_DIM))
        outs["p_mv"].append(mv.reshape(bp, MEM_LEN, M_HEADS, M_HEAD_DIM))
        smk = cache_mem_k[l].reshape(bs, MEM_LEN, M_WIDTH)
        smv = cache_mem_v[l].reshape(bs, MEM_LEN, M_WIDTH)
        if l % 2 == 0:
            pw, pb = _even_weights(w_in_even[i], b_in_even[i])
            args = (rel_bias_table[i], b_conv_w[i], b_conv_b[i][None, :])
            mix_p, ak, av, bc = _even_layer(xp, pw, pb, mk, mv, *args,
                                            jnp.zeros((bp, SUBLANES, B_WIDTH), F32), None, seq=sp)
            mix_s, sk, sv, sb = _even_layer(xs, pw, pb, smk, smv, *args,
                                            _pad_state_rows(state_b_conv[i]), (cache_a_k[i], cache_a_v[i]), seq=ss)
            outs["p_ak"].append(ak); outs["p_av"].append(av); outs["p_bc"].append(bc)
            outs["s_ak"].append(sk); outs["s_av"].append(sv); outs["s_bc"].append(sb)
        else:
            ws = _odd_weights(w_in_odd[i], b_in_odd[i])
            args = (d_conv_w[i], d_conv_b[i], d_A_log[i], d_skip[i], d_norm_w[i])
            zeros = lambda *shape: jnp.zeros(shape, F32)
            mix_p, cc, cn, cm, ds, dc = _odd_layer(
                xp, *ws, mk, mv, *args,
                zeros(bp, C_HEADS, C_HEAD_DIM, C_HEAD_DIM), zeros(bp, C_HEADS, C_HEAD_DIM), zeros(bp, C_HEADS),
                zeros(bp, D_HEADS, D_HEAD_DIM, D_STATE), zeros(bp, D_CONV - 1, D_XBC), seq=sp)
            mix_s, scc, scn, scm, sds, sdc = _odd_layer(
                xs, *ws, smk, smv, *args,
                state_c_C[i], state_c_n[i], state_c_m[i], state_d_ssm[i], state_d_conv[i], seq=ss)
            outs["p_cc"].append(cc); outs["p_cn"].append(cn); outs["p_cm"].append(cm)
            outs["p_ds"].append(ds); outs["p_dc"].append(dc)
            outs["s_cc"].append(scc); outs["s_cn"].append(scn); outs["s_cm"].append(scm)
            outs["s_ds"].append(sds); outs["s_dc"].append(sdc)
        wo = w_out[l].astype(BF16)
        xp = outproj_ln(mix_p, wo, xp, ln_g[l][None, :], ln_b[l][None, :], tm=256)
        xs = outproj_ln(mix_s, wo, xs, ln_g[l][None, :], ln_b[l][None, :], tm=256)
    st = jnp.stack
    return (xp.reshape(bp, sp, D_MODEL), xs.reshape(bs, ss, D_MODEL),
            st(outs["p_mk"]), st(outs["p_mv"]), st(outs["p_ak"]), st(outs["p_av"]), st(outs["p_bc"]),
            st(outs["p_cc"]), st(outs["p_cn"]), st(outs["p_cm"]), st(outs["p_ds"]), st(outs["p_dc"]),
            st(outs["s_ak"]), st(outs["s_av"]), st(outs["s_bc"]),
            st(outs["s_cc"]), st(outs["s_cn"]), st(outs["s_cm"]), st(outs["s_ds"]), st(outs["s_dc"]))
```

```python
import functools

import jax
import jax.numpy as jnp
from jax import lax
from jax.experimental import pallas as pl
from jax.experimental.pallas import tpu as pltpu

F32 = jnp.float32
BF16 = jnp.bfloat16

D_MODEL = 2048
DEPTH = 4
CHUNK = 64
ALPHA = (2 * DEPTH) ** 0.25
EPS = 1e-5
NEG = -1e30

A_WIDTH = D_MODEL // 2
A_HEAD_DIM = 128
A_HEADS = A_WIDTH // A_HEAD_DIM
A_BAND = 8
A_WINDOW = A_BAND * CHUNK
REL_CLIP = 128
B_WIDTH = D_MODEL // 2
B_CONV = 3
C_WIDTH = D_MODEL // 2
C_HEAD_DIM = 128
C_HEADS = C_WIDTH // C_HEAD_DIM
D_WIDTH = D_MODEL // 2
D_HEAD_DIM = 64
D_HEADS = D_WIDTH // D_HEAD_DIM
D_GROUPS = 2
D_HG = D_HEADS // D_GROUPS
D_STATE = 128
D_CONV = 4
D_BC = 2 * D_GROUPS * D_STATE
D_XBC = D_WIDTH + D_BC
MEM_LEN = 256
M_WIDTH = D_MODEL // 4
M_HEADS = 4
M_HEAD_DIM = M_WIDTH // M_HEADS
MIX_WIDTH = A_WIDTH + B_WIDTH + M_WIDTH

LANES = 128
SUBLANES = 8
VMEM_LIMIT_BYTES = 56 * 1024 * 1024

EVEN_QA, EVEN_KA, EVEN_VA, EVEN_BX, EVEN_BGATE, EVEN_CGATE = 3, 4, 5, 6, 7, 8
ODD_QC, ODD_KC, ODD_VC, ODD_OC, ODD_XD = 3, 4, 5, 6, 7
QM_BLOCK = MIX_WIDTH // M_WIDTH
ODD_BC_BLOCK = (MIX_WIDTH + M_WIDTH + 5 * 1024) // D_BC
GATE_COLS = LANES
IG_OFF, FG_OFF, DT_OFF = 0, C_HEADS, 2 * C_HEADS
N_GATES = 2 * C_HEADS + D_HEADS

SCAN_BLOCK = 256
Q_SUB = 256
K_SUB = Q_SUB + A_WINDOW
BAND_ROWS = (A_BAND + 1) * CHUNK


def _params(semantics):
    return pltpu.CompilerParams(dimension_semantics=semantics, vmem_limit_bytes=VMEM_LIMIT_BYTES)


def _nt_dot(a, b):
    return lax.dot_general(a, b, (((1,), (1,)), ((), ())), preferred_element_type=F32)


def _tn_dot(a, b):
    return lax.dot_general(a, b, (((0,), (0,)), ((), ())), preferred_element_type=F32)


def _dot(a, b):
    return jnp.dot(a, b, preferred_element_type=F32)


def _silu(x):
    return x * jax.nn.sigmoid(x)


def _softplus(x):
    return jnp.maximum(x, 0.0) + jnp.log1p(jnp.exp(-jnp.abs(x)))


def _log_sigmoid(x):
    return -_softplus(-x)


def _softmax_pv(s, v):
    m = jnp.max(s, axis=-1, keepdims=True)
    p = jnp.exp(s - m)
    inv = 1.0 / jnp.sum(p, axis=-1, keepdims=True)
    return _dot(p.astype(BF16), v) * inv


def _linear_body(x_ref, w_ref, b_ref, o_ref, xb_ref):
    @pl.when(pl.program_id(1) == 0)
    def _():
        xb_ref[...] = x_ref[...].astype(BF16)

    o_ref[...] = _dot(xb_ref[...], w_ref[...]) + b_ref[...]


def linear(x, w, b, layer, *, tm, tn):
    m, k = x.shape
    n = w.shape[2]
    assert m % tm == 0 and n % tn == 0, (m, n, tm, tn)
    return pl.pallas_call(
        _linear_body,
        out_shape=jax.ShapeDtypeStruct((m, n), F32),
        grid=(m // tm, n // tn),
        in_specs=[
            pl.BlockSpec((tm, k), lambda i, j: (i, 0)),
            pl.BlockSpec((None, k, tn), lambda i, j: (layer, 0, j)),
            pl.BlockSpec((None, 1, tn), lambda i, j: (layer, 0, j)),
        ],
        out_specs=pl.BlockSpec((tm, tn), lambda i, j: (i, j)),
        scratch_shapes=[pltpu.VMEM((tm, k), BF16)],
        compiler_params=_params(("parallel", "arbitrary")),
        name="linear",
    )(x, w, b)


def _table_select(tab_ref, h, idx):
    def step(t, acc):
        return jnp.where(idx == t, tab_ref[h, t], acc)

    return lax.fori_loop(0, 2 * REL_CLIP + 1, step, jnp.zeros(idx.shape, F32))


def _relbias_body(tab_ref, o_ref, *, offset):
    ql, kl = o_ref.shape[1], o_ref.shape[2]
    qi = lax.broadcasted_iota(jnp.int32, (ql, kl), 0)
    ki = lax.broadcasted_iota(jnp.int32, (ql, kl), 1)
    idx = jnp.clip(qi + offset - ki, -REL_CLIP, REL_CLIP) + REL_CLIP
    o_ref[0] = _table_select(tab_ref, pl.program_id(0), idx)


def rel_bias(table, q_len, k_len, offset):
    heads = table.shape[0]
    return pl.pallas_call(
        functools.partial(_relbias_body, offset=offset),
        out_shape=jax.ShapeDtypeStruct((heads, q_len, k_len), F32),
        grid=(heads,),
        in_specs=[pl.BlockSpec(memory_space=pltpu.SMEM)],
        out_specs=pl.BlockSpec((1, q_len, k_len), lambda h: (h, 0, 0)),
        compiler_params=_params(("arbitrary",)),
        name="rel_bias",
    )(table)


def _relbias_band_body(tab_ref, o_ref):
    period = 2 * A_WINDOW
    u = lax.broadcasted_iota(jnp.int32, (SUBLANES, period), 1)
    d = jnp.where(u < K_SUB, u, u - period)
    idx = jnp.clip(A_WINDOW - d, -REL_CLIP, REL_CLIP) + REL_CLIP
    g = _table_select(tab_ref, pl.program_id(0), idx)
    full = pltpu.roll(jnp.broadcast_to(g[0:1], (Q_SUB, period)), 0, 1, stride=1, stride_axis=0)
    r = lax.broadcasted_iota(jnp.int32, (Q_SUB, K_SUB), 0)
    w = lax.broadcasted_iota(jnp.int32, (Q_SUB, K_SUB), 1)
    first = (r // CHUNK) * CHUNK
    in_band = (w >= first) & (w < first + BAND_ROWS)
    o_ref[0] = jnp.where(in_band, full[:, :K_SUB], NEG)


def rel_bias_band(table):
    heads = table.shape[0]
    return pl.pallas_call(
        _relbias_band_body,
        out_shape=jax.ShapeDtypeStruct((heads, Q_SUB, K_SUB), F32),
        grid=(heads,),
        in_specs=[pl.BlockSpec(memory_space=pltpu.SMEM)],
        out_specs=pl.BlockSpec((1, Q_SUB, K_SUB), lambda h: (h, 0, 0)),
        compiler_params=_params(("arbitrary",)),
        name="rel_bias_band",
    )(table)


def _band_prompt_body(q_ref, k_ref, v_ref, bias_ref, o_ref, kwin, vwin):
    i = pl.program_id(0)
    tq = q_ref.shape[0]
    scale = A_HEAD_DIM ** -0.5

    @pl.when(i == 0)
    def _():
        kwin[0:tq] = jnp.zeros((tq, A_WIDTH), BF16)
        vwin[0:tq] = jnp.zeros((tq, A_WIDTH), BF16)

    @pl.when(i > 0)
    def _():
        kwin[0:tq] = kwin[tq:2 * tq]
        vwin[0:tq] = vwin[tq:2 * tq]

    kwin[tq:2 * tq] = k_ref[...].astype(BF16)
    vwin[tq:2 * tq] = v_ref[...].astype(BF16)

    col = lax.broadcasted_iota(jnp.int32, (1, K_SUB), 1)
    unreal = jnp.where(i == 0, tq, 0)
    for r0 in range(0, tq, Q_SUB):
        neg = jnp.where(col + r0 < unreal, NEG, 0.0)
        for h in range(A_HEADS):
            hs = slice(h * A_HEAD_DIM, (h + 1) * A_HEAD_DIM)
            qh = q_ref[r0:r0 + Q_SUB, hs].astype(BF16)
            s = _nt_dot(qh, kwin[r0:r0 + K_SUB, hs]) * scale + bias_ref[h] + neg
            o_ref[r0:r0 + Q_SUB, hs] = _softmax_pv(s, vwin[r0:r0 + K_SUB, hs])


def band_attention_prompt(proj, bias, *, tq):
    s = proj.shape[0]
    assert tq >= A_WINDOW and tq % Q_SUB == 0 and Q_SUB % CHUNK == 0 and s % tq == 0
    return pl.pallas_call(
        _band_prompt_body,
        out_shape=jax.ShapeDtypeStruct((s, A_WIDTH), F32),
        grid=(s // tq,),
        in_specs=[
            pl.BlockSpec((tq, A_WIDTH), lambda i: (i, EVEN_QA)),
            pl.BlockSpec((tq, A_WIDTH), lambda i: (i, EVEN_KA)),
            pl.BlockSpec((tq, A_WIDTH), lambda i: (i, EVEN_VA)),
            pl.BlockSpec((A_HEADS, Q_SUB, K_SUB), lambda i: (0, 0, 0)),
        ],
        out_specs=pl.BlockSpec((tq, A_WIDTH), lambda i: (i, 0)),
        scratch_shapes=[pltpu.VMEM((2 * tq, A_WIDTH), BF16), pltpu.VMEM((2 * tq, A_WIDTH), BF16)],
        compiler_params=_params(("arbitrary",)),
        name="band_attention_prompt",
    )(proj, proj, proj, bias)


def _band_sample_body(q_ref, nk_ref, nv_ref, ck_ref, cv_ref, bc_ref, bn_ref, o_ref):
    scale = A_HEAD_DIM ** -0.5
    for h in range(A_HEADS):
        hs = slice(h * A_HEAD_DIM, (h + 1) * A_HEAD_DIM)
        qh = q_ref[:, hs].astype(BF16)
        s1 = _nt_dot(qh, ck_ref[:, hs].astype(BF16)) * scale + bc_ref[h]
        s2 = _nt_dot(qh, nk_ref[:, hs].astype(BF16)) * scale + bn_ref[h]
        m = jnp.maximum(jnp.max(s1, axis=-1, keepdims=True), jnp.max(s2, axis=-1, keepdims=True))
        p1 = jnp.exp(s1 - m)
        p2 = jnp.exp(s2 - m)
        inv = 1.0 / (jnp.sum(p1, axis=-1, keepdims=True) + jnp.sum(p2, axis=-1, keepdims=True))
        o = _dot(p1.astype(BF16), cv_ref[:, hs].astype(BF16)) + _dot(p2.astype(BF16), nv_ref[:, hs].astype(BF16))
        o_ref[:, hs] = o * inv


def band_attention_sample(proj, cache_k, cache_v, layer, bias_cache, bias_new, *, t):
    _, b, lc, _ = cache_k.shape
    cache = pl.BlockSpec((None, None, lc, A_WIDTH), lambda i: (layer, i, 0, 0))
    return pl.pallas_call(
        _band_sample_body,
        out_shape=jax.ShapeDtypeStruct((b * t, A_WIDTH), F32),
        grid=(b,),
        in_specs=[
            pl.BlockSpec((t, A_WIDTH), lambda i: (i, EVEN_QA)),
            pl.BlockSpec((t, A_WIDTH), lambda i: (i, EVEN_KA)),
            pl.BlockSpec((t, A_WIDTH), lambda i: (i, EVEN_VA)),
            cache, cache,
            pl.BlockSpec((A_HEADS, t, lc), lambda i: (0, 0, 0)),
            pl.BlockSpec((A_HEADS, t, t), lambda i: (0, 0, 0)),
        ],
        out_specs=pl.BlockSpec((t, A_WIDTH), lambda i: (i, 0)),
        compiler_params=_params(("parallel",)),
        name="band_attention_sample",
    )(proj, proj, proj, cache_k, cache_v, bias_cache, bias_new)


def _mem_attn_body(q_ref, mk_ref, mv_ref, o_ref):
    scale = M_HEAD_DIM ** -0.5
    for h in range(M_HEADS):
        hs = slice(h * M_HEAD_DIM, (h + 1) * M_HEAD_DIM)
        s = _nt_dot(q_ref[:, hs].astype(BF16), mk_ref[:, hs].astype(BF16)) * scale
        o_ref[:, hs] = _softmax_pv(s, mv_ref[:, hs].astype(BF16))


def mem_attention(proj, mk, mv, kmap, vmap, *, batch, seq, tl):
    nt = seq // tl
    mem = lambda f: pl.BlockSpec((None, None, MEM_LEN, M_WIDTH), lambda i, j: f(i))
    return pl.pallas_call(
        _mem_attn_body,
        out_shape=jax.ShapeDtypeStruct((batch * seq, M_WIDTH), F32),
        grid=(batch, nt),
        in_specs=[pl.BlockSpec((tl, M_WIDTH), lambda i, j: (i * nt + j, QM_BLOCK)), mem(kmap), mem(vmap)],
        out_specs=pl.BlockSpec((tl, M_WIDTH), lambda i, j: (i * nt + j, 0)),
        compiler_params=_params(("parallel", "parallel")),
        name="mem_attention",
    )(proj, mk, mv)


def _shift_rows(x, halo, s):
    rolled = pltpu.roll(x, s, 0)
    halo_rolled = pltpu.roll(halo, s, 0)
    row = lax.broadcasted_iota(jnp.int32, halo.shape, 0)
    top = jnp.where(row < s, halo_rolled, rolled[0:SUBLANES])
    if x.shape[0] == SUBLANES:
        return top
    return jnp.concatenate([top, rolled[SUBLANES:]], axis=0)


def _causal_dwconv(x, halo, w_ref, b_ref):
    width = w_ref.shape[0]
    y = b_ref[...]
    for j in range(width):
        s = width - 1 - j
        xs = x if s == 0 else _shift_rows(x, halo, s)
        y = y + xs * w_ref[j:j + 1, :]
    return y


def _residual_ln(acc, x_ref, g_ref, b_ref, o_ref):
    y = ALPHA * x_ref[...] + acc
    mu = jnp.mean(y, axis=-1, keepdims=True)
    d = y - mu
    var = jnp.mean(d * d, axis=-1, keepdims=True)
    o_ref[...] = d * lax.rsqrt(var + EPS) * g_ref[...] + b_ref[...]


def _mix_matmul(parts, w_ref):
    acc, r0 = None, 0
    for p in parts:
        term = _dot(p.astype(BF16), w_ref[r0:r0 + p.shape[1], :])
        acc = term if acc is None else acc + term
        r0 += p.shape[1]
    return acc


def _even_tail_body(gate_ref, bx_ref, bg_ref, cg_ref, ya_ref, ym_ref, cw_ref, cb_ref, halo_ref,
                    x_ref, w_ref, g_ref, b_ref, o_ref, tail_ref, carry):
    @pl.when(pl.program_id(1) == 0)
    def _():
        carry[...] = halo_ref[...]

    tm = bx_ref.shape[0]
    u = cg_ref[...] * bx_ref[...]
    yb = bg_ref[...] * _causal_dwconv(u, carry[...], cw_ref, cb_ref)
    last = u[tm - SUBLANES:tm]
    carry[...] = last
    tail_ref[...] = last
    sg = _silu(gate_ref[...])
    parts = (ya_ref[...] * sg[:, 0:A_WIDTH],
             yb * sg[:, A_WIDTH:A_WIDTH + B_WIDTH],
             ym_ref[...] * sg[:, A_WIDTH + B_WIDTH:MIX_WIDTH])
    _residual_ln(_mix_matmul(parts, w_ref), x_ref, g_ref, b_ref, o_ref)


def _layer_param(shape, layer):
    return pl.BlockSpec((None,) + shape, lambda i, j: (layer,) + (0,) * len(shape))


def _out_weight(layer):
    return pl.BlockSpec((None, MIX_WIDTH, D_MODEL), lambda i, j: (layer, 0, 0), pipeline_mode=pl.Buffered(1))


def even_tail(proj, ya, ym, x, conv_w, conv_b, halo, halo_layer, w_out, ln_g, ln_b, conv_layer, layer, *, seq, tm):
    b = halo.shape[1]
    nt = seq // tm
    row = lambda i, j: i * nt + j
    rows = lambda width, col: pl.BlockSpec((tm, width), lambda i, j: (row(i, j), col))
    return pl.pallas_call(
        _even_tail_body,
        out_shape=(jax.ShapeDtypeStruct((b * seq, D_MODEL), F32),
                   jax.ShapeDtypeStruct((b, SUBLANES, B_WIDTH), F32)),
        grid=(b, nt),
        in_specs=[
            rows(MIX_WIDTH, 0), rows(B_WIDTH, EVEN_BX), rows(B_WIDTH, EVEN_BGATE), rows(B_WIDTH, EVEN_CGATE),
            rows(A_WIDTH, 0), rows(M_WIDTH, 0),
            _layer_param((B_CONV, B_WIDTH), conv_layer), _layer_param((1, B_WIDTH), conv_layer),
            pl.BlockSpec((None, None, SUBLANES, B_WIDTH), lambda i, j: (halo_layer, i, 0, 0)),
            rows(D_MODEL, 0), _out_weight(layer), _layer_param((1, D_MODEL), layer), _layer_param((1, D_MODEL), layer),
        ],
        out_specs=(rows(D_MODEL, 0), pl.BlockSpec((None, SUBLANES, B_WIDTH), lambda i, j: (i, 0, 0))),
        scratch_shapes=[pltpu.VMEM((SUBLANES, B_WIDTH), F32)],
        compiler_params=_params(("arbitrary", "arbitrary")),
        name="even_tail",
    )(proj, proj, proj, proj, ya, ym, conv_w, conv_b, halo, x, w_out, ln_g, ln_b)


def _odd_tail_body(gate_ref, oc_ref, hc_ref, yd_ref, ym_ref, nw_ref, x_ref, w_ref, g_ref, b_ref, o_ref):
    sg = _silu(gate_ref[...])
    yc = hc_ref[...] * jax.nn.sigmoid(oc_ref[...]) * sg[:, 0:C_WIDTH]
    z = yd_ref[...] * sg[:, C_WIDTH:C_WIDTH + D_WIDTH]
    yd = z * lax.rsqrt(jnp.mean(z * z, axis=-1, keepdims=True) + EPS) * nw_ref[...]
    parts = (yc, yd, ym_ref[...] * sg[:, C_WIDTH + D_WIDTH:MIX_WIDTH])
    _residual_ln(_mix_matmul(parts, w_ref), x_ref, g_ref, b_ref, o_ref)


def odd_tail(proj, hc, yd, ym, x, norm_w, norm_layer, w_out, ln_g, ln_b, layer, *, tm):
    m = x.shape[0]
    rows = lambda width, col: pl.BlockSpec((tm, width), lambda i, j: (i, col))
    return pl.pallas_call(
        _odd_tail_body,
        out_shape=jax.ShapeDtypeStruct((m, D_MODEL), F32),
        grid=(m // tm, 1),
        in_specs=[
            rows(MIX_WIDTH, 0), rows(C_WIDTH, ODD_OC), rows(C_WIDTH, 0), rows(D_WIDTH, 0), rows(M_WIDTH, 0),
            _layer_param((1, D_WIDTH), norm_layer),
            rows(D_MODEL, 0), _out_weight(layer), _layer_param((1, D_MODEL), layer), _layer_param((1, D_MODEL), layer),
        ],
        out_specs=rows(D_MODEL, 0),
        compiler_params=_params(("parallel", "arbitrary")),
        name="odd_tail",
    )(proj, proj, hc, yd, ym, norm_w, x, w_out, ln_g, ln_b)


def _tri_masks(t):
    ri = lax.broadcasted_iota(jnp.int32, (t, t), 0)
    ci = lax.broadcasted_iota(jnp.int32, (t, t), 1)
    return ri >= ci, ri <= ci


def _cumsum_cols(lower, x):
    return jnp.dot(lower.astype(F32), x, precision=lax.Precision.HIGHEST, preferred_element_type=F32)


def _cumsum_rows(upper, x):
    return jnp.dot(x, upper.astype(F32), precision=lax.Precision.HIGHEST, preferred_element_type=F32)


def _state_spec(shape, layer):
    return pl.BlockSpec((None, 1) + shape, lambda i, j: (layer, i) + (0,) * len(shape))


def _state_out_spec(shape):
    return pl.BlockSpec((1,) + shape, lambda i, j: (i,) + (0,) * len(shape))


def _mlstm_body(q_ref, k_ref, v_ref, gc_ref, gr_ref, c0_ref, n0_ref, m0_ref,
                h_ref, c_ref, n_ref, m_ref):
    @pl.when(pl.program_id(1) == 0)
    def _():
        c_ref[...] = c0_ref[...]
        n_ref[...] = n0_ref[...]
        m_ref[...] = m0_ref[...]

    t = q_ref.shape[0]
    scale = C_HEAD_DIM ** -0.5
    lower, upper = _tri_masks(t)
    gc = gc_ref[...]
    gr = gr_ref[...]
    ig_c = gc[:, IG_OFF:IG_OFF + C_HEADS]
    ig_r = gr[IG_OFF:IG_OFF + C_HEADS, :]
    b_c = _cumsum_cols(lower, _log_sigmoid(gc))[:, FG_OFF:FG_OFF + C_HEADS]
    b_r = _cumsum_rows(upper, _log_sigmoid(gr))[FG_OFF:FG_OFF + C_HEADS, :]
    for h in range(C_HEADS):
        hs = slice(h * C_HEAD_DIM, (h + 1) * C_HEAD_DIM)
        bc = b_c[:, h:h + 1]
        ms = m_ref[0, h:h + 1, 0:1]
        dlog = jnp.where(lower, bc - b_r[h:h + 1, :] + ig_r[h:h + 1, :], -jnp.inf)
        g = bc + ms
        m = jnp.maximum(g, jnp.max(dlog, axis=1, keepdims=True))
        qf = q_ref[:, hs]
        kf = k_ref[:, hs] * scale
        qb = qf.astype(BF16)
        kb = kf.astype(BF16)
        vb = v_ref[:, hs].astype(BF16)
        w = _nt_dot(qb, kb) * jnp.exp(dlog - m)
        gw = jnp.exp(g - m)
        cs = c_ref[0, h]
        ns = n_ref[0, h:h + 1, :]
        num = _dot(w.astype(BF16), vb) + gw * _dot(qb, cs.astype(BF16))
        nq = jnp.sum(w, axis=1, keepdims=True) + gw * jnp.sum(qf * ns, axis=1, keepdims=True)
        h_ref[:, hs] = num * (1.0 / jnp.maximum(jnp.abs(nq), jnp.exp(-m)))
        bt = bc[t - 1:t, :]
        wlog = bt - bc + ig_c[:, h:h + 1]
        m_new = jnp.maximum(bt + ms, jnp.max(wlog, axis=0, keepdims=True))
        wj = jnp.exp(wlog - m_new)
        decay = jnp.exp(bt + ms - m_new)
        kw = kf * wj
        c_ref[0, h] = decay * cs + _tn_dot(kw.astype(BF16), vb)
        n_ref[0, h:h + 1, :] = decay * ns + jnp.sum(kw, axis=0, keepdims=True)
        m_ref[0, h:h + 1, :] = jnp.broadcast_to(m_new, (1, C_HEAD_DIM))


def mlstm(proj, gates_c, gates_r, c0, n0, m0, layer, *, batch, seq, t):
    nb = seq // t
    row = lambda i, j: i * nb + j
    rows = lambda width, col: pl.BlockSpec((t, width), lambda i, j: (row(i, j), col))
    shapes = (c0.shape[2:], n0.shape[2:], m0.shape[2:])
    return pl.pallas_call(
        _mlstm_body,
        out_shape=(jax.ShapeDtypeStruct((batch * seq, C_WIDTH), F32),)
        + tuple(jax.ShapeDtypeStruct((batch,) + s, F32) for s in shapes),
        grid=(batch, nb),
        in_specs=[
            rows(C_WIDTH, ODD_QC), rows(C_WIDTH, ODD_KC), rows(C_WIDTH, ODD_VC), rows(GATE_COLS, 0),
            pl.BlockSpec((None, N_GATES, t), lambda i, j: (row(i, j), 0, 0)),
        ] + [_state_spec(s, layer) for s in shapes],
        out_specs=(rows(C_WIDTH, 0),) + tuple(_state_out_spec(s) for s in shapes),
        compiler_params=_params(("arbitrary", "arbitrary")),
        name="mlstm",
    )(proj, proj, proj, gates_c, gates_r, c0, n0, m0)


def _ssd_body(xd_ref, bc_ref, gc_ref, gr_ref, wx_ref, wbc_ref, bx_ref, bbc_ref,
              alr_ref, alc_ref, dskip_ref, hx0_ref, hbc0_ref, s0_ref,
              y_ref, s_ref, hxo_ref, hbco_ref, hx, hbc):
    @pl.when(pl.program_id(1) == 0)
    def _():
        s_ref[...] = s0_ref[...]
        hx[...] = hx0_ref[0]
        hbc[...] = hbc0_ref[0]

    t = xd_ref.shape[0]
    xr = xd_ref[...]
    br = bc_ref[...]
    xs = _silu(_causal_dwconv(xr, hx[...], wx_ref, bx_ref))
    bcs = _silu(_causal_dwconv(br, hbc[...], wbc_ref, bbc_ref))
    hx[...] = xr[t - SUBLANES:t]
    hbc[...] = br[t - SUBLANES:t]
    hxo_ref[0] = xr[t - SUBLANES:t]
    hbco_ref[0] = br[t - SUBLANES:t]

    lower, upper = _tri_masks(t)
    dt_c = _softplus(gc_ref[:, DT_OFF:DT_OFF + D_HEADS])
    dt_r = _softplus(gr_ref[DT_OFF:DT_OFF + D_HEADS, :])
    a_c = _cumsum_cols(lower, dt_c * -jnp.exp(alr_ref[...]))
    a_r = _cumsum_rows(upper, dt_r * -jnp.exp(alc_ref[...]))
    for g in range(D_GROUPS):
        bmg = bcs[:, g * D_STATE:(g + 1) * D_STATE].astype(BF16)
        cmg = bcs[:, (D_GROUPS + g) * D_STATE:(D_GROUPS + g + 1) * D_STATE].astype(BF16)
        cb = _nt_dot(cmg, bmg)
        for hh in range(D_HG):
            h = g * D_HG + hh
            hs = slice(h * D_HEAD_DIM, (h + 1) * D_HEAD_DIM)
            ac = a_c[:, h:h + 1]
            dtc = dt_c[:, h:h + 1]
            lw = jnp.exp(jnp.where(lower, ac - a_r[h:h + 1, :], -jnp.inf))
            xh = xs[:, hs]
            sh = s_ref[0, h]
            y = _dot((cb * lw).astype(BF16), (xh * dtc).astype(BF16))
            y = y + jnp.exp(ac) * _nt_dot(cmg, sh.astype(BF16))
            y_ref[:, hs] = y + dskip_ref[:, hs] * xh
            at = ac[t - 1:t, :]
            wj = jnp.exp(at - ac) * dtc
            s_ref[0, h] = jnp.exp(at) * sh + _tn_dot((xh * wj).astype(BF16), bmg)


def ssd(proj, gates_c, gates_r, wx, wbc, bx, bbc, alog_r, alog_c, dskip, hx0, hbc0, s0, layer, state_layer,
        *, batch, seq, t):
    nb = seq // t
    row = lambda i, j: i * nb + j
    rows = lambda width, col: pl.BlockSpec((t, width), lambda i, j: (row(i, j), col))
    param = lambda a: _layer_param(a.shape[1:], layer)
    shapes = (s0.shape[2:], hx0.shape[2:], hbc0.shape[2:])
    return pl.pallas_call(
        _ssd_body,
        out_shape=(jax.ShapeDtypeStruct((batch * seq, D_WIDTH), F32),)
        + tuple(jax.ShapeDtypeStruct((batch,) + s, F32) for s in shapes),
        grid=(batch, nb),
        in_specs=[
            rows(D_WIDTH, ODD_XD), rows(D_BC, ODD_BC_BLOCK),
            rows(GATE_COLS, 0), pl.BlockSpec((None, N_GATES, t), lambda i, j: (row(i, j), 0, 0)),
            param(wx), param(wbc), param(bx), param(bbc), param(alog_r), param(alog_c), param(dskip),
            _state_spec(shapes[1], state_layer), _state_spec(shapes[2], state_layer), _state_spec(shapes[0], state_layer),
        ],
        out_specs=(rows(D_WIDTH, 0),) + tuple(_state_out_spec(s) for s in shapes),
        scratch_shapes=[pltpu.VMEM((SUBLANES, D_WIDTH), F32), pltpu.VMEM((SUBLANES, D_BC), F32)],
        compiler_params=_params(("arbitrary", "arbitrary")),
        name="ssd",
    )(proj, proj, gates_c, gates_r, wx, wbc, bx, bbc, alog_r, alog_c, dskip, hx0, hbc0, s0)


def _pad_state_rows(buf):
    pad = jnp.zeros(buf.shape[:-2] + (SUBLANES - buf.shape[-2], buf.shape[-1]), F32)
    return jnp.concatenate([pad, buf], axis=-2)


def _row_gates(gates_c, t):
    m = gates_c.shape[0]
    return gates_c[:, :N_GATES].reshape(m // t, t, N_GATES).transpose(0, 2, 1)


def _even_weights(w_in, b_in):
    off = A_WIDTH * 3 + B_WIDTH * 3
    order = lambda a: jnp.concatenate([a[..., off + M_WIDTH:], a[..., off:off + M_WIDTH], a[..., :off]], axis=-1)
    return order(w_in).astype(BF16), order(b_in)[:, None, :]


def _odd_weights(w_in, b_in):
    c4 = 4 * C_WIDTH
    g0 = c4 + 2 * C_HEADS
    d0 = g0 + D_XBC
    q0 = d0 + D_HEADS
    t0 = q0 + M_WIDTH
    big = lambda a: jnp.concatenate([a[..., t0:], a[..., q0:t0], a[..., :c4], a[..., g0:d0]], axis=-1)
    small = lambda a: jnp.concatenate(
        [a[..., c4:g0], a[..., d0:q0], jnp.zeros(a.shape[:-1] + (GATE_COLS - N_GATES,), a.dtype)], axis=-1)
    return big(w_in).astype(BF16), big(b_in)[:, None, :], small(w_in).astype(BF16), small(b_in)[:, None, :]


def _tiles(m):
    return dict(tm=min(m, 1024), tn=512)


def _heads(a, batch, seq, col, last):
    part = a.reshape(batch, seq, a.shape[1])[:, seq - last:, col * A_WIDTH:(col + 1) * A_WIDTH]
    return part.reshape(batch, last, A_HEADS, A_HEAD_DIM)


def kernel(x_prompt, x_sample, mem_prompt, cache_a_k, cache_a_v, cache_mem_k, cache_mem_v, state_b_conv, state_c_C, state_c_n, state_c_m, state_d_ssm, state_d_conv, w_in_even, b_in_even, w_in_odd, b_in_odd, w_out, w_mem_kv, ln_g, ln_b, rel_bias_table, b_conv_w, b_conv_b, d_conv_w, d_conv_b, d_A_log, d_skip, d_norm_w):
    bp, sp, _ = x_prompt.shape
    bs, ss, _ = x_sample.shape
    n_even, n_odd = w_in_even.shape[0], w_in_odd.shape[0]
    lc = cache_a_k.shape[2]
    xp = x_prompt.reshape(bp * sp, D_MODEL)
    xs = x_sample.reshape(bs * ss, D_MODEL)
    mem = mem_prompt.reshape(bp * MEM_LEN, D_MODEL)

    we, be = _even_weights(w_in_even, b_in_even)
    wo_big, bo_big, wo_small, bo_small = _odd_weights(w_in_odd, b_in_odd)
    w_out_b = w_out.astype(BF16)
    w_mem_b = w_mem_kv.astype(BF16)
    zero_bias = jnp.zeros((1, 1, 2 * M_WIDTH), F32)
    ln_g3, ln_b3 = ln_g[:, None, :], ln_b[:, None, :]
    conv_b3 = b_conv_b[:, None, :]
    norm_w3 = d_norm_w[:, None, :]
    d_wx, d_wbc = d_conv_w[:, :, :D_WIDTH], d_conv_w[:, :, D_WIDTH:]
    d_bx, d_bbc = d_conv_b[:, None, :D_WIDTH], d_conv_b[:, None, D_WIDTH:]
    alog_r, alog_c = d_A_log[:, None, :], d_A_log[:, :, None]
    dskip = jnp.repeat(d_skip, D_HEAD_DIM, axis=-1)[:, None, :]

    cak = cache_a_k.reshape(n_even, bs, lc, A_WIDTH)
    cav = cache_a_v.reshape(n_even, bs, lc, A_WIDTH)
    cmk = cache_mem_k.reshape(DEPTH, bs, MEM_LEN, M_WIDTH)
    cmv = cache_mem_v.reshape(DEPTH, bs, MEM_LEN, M_WIDTH)
    s_halo_b = _pad_state_rows(state_b_conv)
    s_halo_d = _pad_state_rows(state_d_conv)
    s_hx0, s_hbc0 = s_halo_d[..., :D_WIDTH], s_halo_d[..., D_WIDTH:]
    s_m0 = jnp.broadcast_to(state_c_m[..., None], state_c_m.shape + (C_HEAD_DIM,))
    zeros = lambda *shape: jnp.zeros((1, bp) + shape, F32)
    p_halo_b, p_hx0, p_hbc0 = zeros(SUBLANES, B_WIDTH), zeros(SUBLANES, D_WIDTH), zeros(SUBLANES, D_BC)
    p_c0, p_n0, p_m0 = zeros(C_HEADS, C_HEAD_DIM, C_HEAD_DIM), zeros(C_HEADS, C_HEAD_DIM), zeros(C_HEADS, C_HEAD_DIM)
    p_s0 = zeros(D_HEADS, D_HEAD_DIM, D_STATE)

    outs = {k: [] for k in ("p_mk", "p_mv", "p_ak", "p_av", "p_bc", "p_cc", "p_cn", "p_cm", "p_ds", "p_dc",
                            "s_ak", "s_av", "s_bc", "s_cc", "s_cn", "s_cm", "s_ds", "s_dc")}
    tp = min(SCAN_BLOCK, sp)
    ts = min(SCAN_BLOCK, ss)
    for l in range(DEPTH):
        i = l // 2
        mkv = linear(mem, w_mem_b, zero_bias.repeat(DEPTH, 0), l, tm=MEM_LEN, tn=512)
        outs["p_mk"].append(mkv[:, :M_WIDTH].reshape(bp, MEM_LEN, M_HEADS, M_HEAD_DIM))
        outs["p_mv"].append(mkv[:, M_WIDTH:].reshape(bp, MEM_LEN, M_HEADS, M_HEAD_DIM))
        mkv4 = mkv.reshape(1, bp, MEM_LEN, 2 * M_WIDTH)
        p_mem = (mkv4, mkv4, lambda b: (0, b, 0, 0), lambda b: (0, b, 0, 1))
        s_mem = (cmk, cmv, lambda b, l=l: (l, b, 0, 0), lambda b, l=l: (l, b, 0, 0))
        if l % 2 == 0:
            proj_p = linear(xp, we, be, i, **_tiles(xp.shape[0]))
            proj_s = linear(xs, we, be, i, **_tiles(xs.shape[0]))
            ya_p = band_attention_prompt(proj_p, rel_bias_band(rel_bias_table[i]), tq=A_WINDOW)
            bias_s = rel_bias(rel_bias_table[i], ss, lc + ss, lc)
            ya_s = band_attention_sample(proj_s, cak, cav, i, bias_s[:, :, :lc], bias_s[:, :, lc:], t=ss)
            ym_p = mem_attention(proj_p, *p_mem, batch=bp, seq=sp, tl=min(sp, 512))
            ym_s = mem_attention(proj_s, *s_mem, batch=bs, seq=ss, tl=min(ss, 512))
            tail = (b_conv_w, conv_b3)
            xp_new, tail_p = even_tail(proj_p, ya_p, ym_p, xp, *tail, p_halo_b, 0, w_out_b, ln_g3, ln_b3, i, l,
                                       seq=sp, tm=min(sp, 256))
            xs_new, tail_s = even_tail(proj_s, ya_s, ym_s, xs, *tail, s_halo_b, i, w_out_b, ln_g3, ln_b3, i, l,
                                       seq=ss, tm=min(ss, 256))
            outs["p_ak"].append(_heads(proj_p, bp, sp, EVEN_KA, A_WINDOW))
            outs["p_av"].append(_heads(proj_p, bp, sp, EVEN_VA, A_WINDOW))
            outs["s_ak"].append(_heads(proj_s, bs, ss, EVEN_KA, ss))
            outs["s_av"].append(_heads(proj_s, bs, ss, EVEN_VA, ss))
            outs["p_bc"].append(tail_p[:, SUBLANES - (B_CONV - 1):])
            outs["s_bc"].append(tail_s[:, SUBLANES - (B_CONV - 1):])
        else:
            ssd_params = (d_wx, d_wbc, d_bx, d_bbc, alog_r, alog_c, dskip)
            res = []
            for x, batch, seq, t, mem_args, states, sl in (
                    (xp, bp, sp, tp, p_mem, (p_c0, p_n0, p_m0, p_hx0, p_hbc0, p_s0), 0),
                    (xs, bs, ss, ts, s_mem, (state_c_C, state_c_n, s_m0, s_hx0, s_hbc0, state_d_ssm), i)):
                c0, n0, m0, hx0, hbc0, s0 = states
                proj = linear(x, wo_big, bo_big, i, **_tiles(x.shape[0]))
                gates_c = linear(x, wo_small, bo_small, i, tm=min(x.shape[0], 1024), tn=GATE_COLS)
                gates_r = _row_gates(gates_c, t)
                hc, cc, cn, cm = mlstm(proj, gates_c, gates_r, c0, n0, m0, sl, batch=batch, seq=seq, t=t)
                yd, s_new, hx, hbc = ssd(proj, gates_c, gates_r, *ssd_params, hx0, hbc0, s0, i, sl,
                                         batch=batch, seq=seq, t=t)
                ym = mem_attention(proj, *mem_args, batch=batch, seq=seq, tl=min(seq, 512))
                x_new = odd_tail(proj, hc, yd, ym, x, norm_w3, i, w_out_b, ln_g3, ln_b3, l, tm=256)
                dbuf = jnp.concatenate([hx, hbc], axis=-1)[:, SUBLANES - (D_CONV - 1):]
                res.append((x_new, cc, cn, cm[:, :, 0], s_new, dbuf))
            (xp_new, *st_p), (xs_new, *st_s) = res
            for key, val in zip(("p_cc", "p_cn", "p_cm", "p_ds", "p_dc"), st_p):
                outs[key].append(val)
            for key, val in zip(("s_cc", "s_cn", "s_cm", "s_ds", "s_dc"), st_s):
                outs[key].append(val)
        xp, xs = xp_new, xs_new
    st = jnp.stack
    return (xp.reshape(bp, sp, D_MODEL), xs.reshape(bs, ss, D_MODEL),
            st(outs["p_mk"]), st(outs["p_mv"]), st(outs["p_ak"]), st(outs["p_av"]), st(outs["p_bc"]),
            st(outs["p_cc"]), st(outs["p_cn"]), st(outs["p_cm"]), st(outs["p_ds"]), st(outs["p_dc"]),
            st(outs["s_ak"]), st(outs["s_av"]), st(outs["s_bc"]),
            st(outs["s_cc"]), st(outs["s_cn"]), st(outs["s_cm"]), st(outs["s_ds"]), st(outs["s_dc"]))
```

```python
import functools

import jax
import jax.numpy as jnp
from jax import lax
from jax.experimental import pallas as pl
from jax.experimental.pallas import tpu as pltpu

F32 = jnp.float32
BF16 = jnp.bfloat16

D_MODEL = 2048
DEPTH = 4
CHUNK = 64
ALPHA = (2 * DEPTH) ** 0.25
EPS = 1e-5
NEG = -1e30

A_WIDTH = D_MODEL // 2
A_HEAD_DIM = 128
A_HEADS = A_WIDTH // A_HEAD_DIM
A_BAND = 8
A_WINDOW = A_BAND * CHUNK
REL_CLIP = 128
B_WIDTH = D_MODEL // 2
B_CONV = 3
C_WIDTH = D_MODEL // 2
C_HEAD_DIM = 128
C_HEADS = C_WIDTH // C_HEAD_DIM
D_WIDTH = D_MODEL // 2
D_HEAD_DIM = 64
D_HEADS = D_WIDTH // D_HEAD_DIM
D_GROUPS = 2
D_HG = D_HEADS // D_GROUPS
D_STATE = 128
D_CONV = 4
D_BC = 2 * D_GROUPS * D_STATE
D_XBC = D_WIDTH + D_BC
MEM_LEN = 256
M_WIDTH = D_MODEL // 4
M_HEADS = 4
M_HEAD_DIM = M_WIDTH // M_HEADS
MIX_WIDTH = A_WIDTH + B_WIDTH + M_WIDTH

LANES = 128
SUBLANES = 8
VMEM_LIMIT_BYTES = 56 * 1024 * 1024

EVEN_QA, EVEN_KA, EVEN_VA, EVEN_BX, EVEN_BGATE, EVEN_CGATE = 3, 4, 5, 6, 7, 8
ODD_QC, ODD_KC, ODD_VC, ODD_OC, ODD_XD = 3, 4, 5, 6, 7
QM_BLOCK = MIX_WIDTH // M_WIDTH
ODD_BC_BLOCK = (MIX_WIDTH + M_WIDTH + 5 * 1024) // D_BC
GATE_COLS = LANES
IG_OFF, FG_OFF, DT_OFF = 0, C_HEADS, 2 * C_HEADS
N_GATES = 2 * C_HEADS + D_HEADS

SCAN_BLOCK = 256
Q_SUB = 256
K_SUB = Q_SUB + A_WINDOW
BAND_ROWS = (A_BAND + 1) * CHUNK
LOG2E = 1.4426950408889634
LINEAR_ROWS = 2048
TAIL_SUB = 128


def _params(semantics):
    return pltpu.CompilerParams(dimension_semantics=semantics, vmem_limit_bytes=VMEM_LIMIT_BYTES)


def _nt_dot(a, b):
    return lax.dot_general(a, b, (((1,), (1,)), ((), ())), preferred_element_type=F32)


def _tn_dot(a, b):
    return lax.dot_general(a, b, (((0,), (0,)), ((), ())), preferred_element_type=F32)


def _dot(a, b):
    return jnp.dot(a, b, preferred_element_type=F32)


def _fine_dot(a, b):
    return jnp.dot(a, b, precision=lax.Precision.HIGHEST, preferred_element_type=F32)


def _fine_nt_dot(a, b):
    return lax.dot_general(a, b, (((1,), (1,)), ((), ())), precision=lax.Precision.HIGHEST,
                           preferred_element_type=F32)


def _silu(x):
    return x * jax.nn.sigmoid(x)


def _softplus(x):
    return jnp.maximum(x, 0.0) + jnp.log1p(jnp.exp(-jnp.abs(x)))


def _log_sigmoid(x):
    return -_softplus(-x)


def _softmax_pv(s, v, exp=jnp.exp):
    m = jnp.max(s, axis=-1, keepdims=True)
    p = exp(s - m)
    inv = 1.0 / jnp.sum(p, axis=-1, keepdims=True)
    return _dot(p.astype(BF16), v) * inv


def _split_bf16(a):
    hi = a.astype(BF16)
    return hi, (a - hi.astype(F32)).astype(BF16)


def _linear_body(x_ref, w_ref, b_ref, o_ref):
    o_ref[...] = _dot(x_ref[...], w_ref[...]) + b_ref[...]


def _linear_fine_body(x_ref, w_ref, b_ref, xl_ref, wl_ref, o_ref, *, fine):
    j = pl.program_id(1)
    o_ref[...] = _dot(x_ref[...], w_ref[...]) + b_ref[...]

    @pl.when((j >= fine[0]) & (j < fine[1]))
    def _():
        o_ref[...] += _dot(xl_ref[...], w_ref[...]) + _dot(x_ref[...], wl_ref[...])


def linear(x, w, b, layer, *, tn=512, fine=None):
    m, k = x.shape
    n = w.shape[2]
    tm = min(m, LINEAR_ROWS)
    assert m % tm == 0 and n % tn == 0, (m, n, tm, tn)
    in_specs = [
        pl.BlockSpec((tm, k), lambda i, j: (i, 0)),
        pl.BlockSpec((None, k, tn), lambda i, j: (layer, 0, j)),
        pl.BlockSpec((None, 1, tn), lambda i, j: (layer, 0, j)),
    ]
    body, args = _linear_body, (x, w, b)
    if fine is not None:
        x_lo, w_lo, j0, j1 = fine
        assert w_lo.shape[2] == (j1 - j0) * tn
        in_specs += [
            pl.BlockSpec((tm, k), lambda i, j: (i, 0)),
            pl.BlockSpec((None, k, tn), lambda i, j: (layer, 0, jnp.clip(j - j0, 0, j1 - j0 - 1))),
        ]
        body, args = functools.partial(_linear_fine_body, fine=(j0, j1)), (x, w, b, x_lo, w_lo)
    return pl.pallas_call(
        body,
        out_shape=jax.ShapeDtypeStruct((m, n), F32),
        grid=(m // tm, n // tn),
        in_specs=in_specs,
        out_specs=pl.BlockSpec((tm, tn), lambda i, j: (i, j)),
        compiler_params=_params(("parallel", "arbitrary")),
        name="linear",
    )(*args)


def _table_select(tab_ref, h, idx):
    def step(t, acc):
        return jnp.where(idx == t, tab_ref[h, t], acc)

    return lax.fori_loop(0, 2 * REL_CLIP + 1, step, jnp.zeros(idx.shape, F32))


def _relbias_body(tab_ref, o_ref, *, offset):
    ql, kl = o_ref.shape[1], o_ref.shape[2]
    qi = lax.broadcasted_iota(jnp.int32, (ql, kl), 0)
    ki = lax.broadcasted_iota(jnp.int32, (ql, kl), 1)
    idx = jnp.clip(qi + offset - ki, -REL_CLIP, REL_CLIP) + REL_CLIP
    o_ref[0] = _table_select(tab_ref, pl.program_id(0), idx)


def rel_bias(table, q_len, k_len, offset):
    heads = table.shape[0]
    return pl.pallas_call(
        functools.partial(_relbias_body, offset=offset),
        out_shape=jax.ShapeDtypeStruct((heads, q_len, k_len), F32),
        grid=(heads,),
        in_specs=[pl.BlockSpec(memory_space=pltpu.SMEM)],
        out_specs=pl.BlockSpec((1, q_len, k_len), lambda h: (h, 0, 0)),
        compiler_params=_params(("arbitrary",)),
        name="rel_bias",
    )(table)


def _relbias_band_body(tab_ref, o_ref):
    period = 2 * A_WINDOW
    u = lax.broadcasted_iota(jnp.int32, (SUBLANES, period), 1)
    d = jnp.where(u < K_SUB, u, u - period)
    idx = jnp.clip(A_WINDOW - d, -REL_CLIP, REL_CLIP) + REL_CLIP
    g = _table_select(tab_ref, pl.program_id(0), idx)
    full = pltpu.roll(jnp.broadcast_to(g[0:1], (Q_SUB, period)), 0, 1, stride=1, stride_axis=0)
    r = lax.broadcasted_iota(jnp.int32, (Q_SUB, K_SUB), 0)
    w = lax.broadcasted_iota(jnp.int32, (Q_SUB, K_SUB), 1)
    first = (r // CHUNK) * CHUNK
    in_band = (w >= first) & (w < first + BAND_ROWS)
    o_ref[0] = jnp.where(in_band, full[:, :K_SUB] * LOG2E, NEG)


def rel_bias_band(table):
    heads = table.shape[0]
    return pl.pallas_call(
        _relbias_band_body,
        out_shape=jax.ShapeDtypeStruct((heads, Q_SUB, K_SUB), F32),
        grid=(heads,),
        in_specs=[pl.BlockSpec(memory_space=pltpu.SMEM)],
        out_specs=pl.BlockSpec((1, Q_SUB, K_SUB), lambda h: (h, 0, 0)),
        compiler_params=_params(("arbitrary",)),
        name="rel_bias_band",
    )(table)


def _band_prompt_body(q_ref, k_ref, v_ref, bias_ref, o_ref, kwin, vwin):
    i = pl.program_id(0)
    tq = q_ref.shape[0]
    scale = A_HEAD_DIM ** -0.5 * LOG2E

    @pl.when(i == 0)
    def _():
        kwin[0:tq] = jnp.zeros((tq, A_WIDTH), BF16)
        vwin[0:tq] = jnp.zeros((tq, A_WIDTH), BF16)

    @pl.when(i > 0)
    def _():
        kwin[0:tq] = kwin[tq:2 * tq]
        vwin[0:tq] = vwin[tq:2 * tq]

    kwin[tq:2 * tq] = k_ref[...].astype(BF16)
    vwin[tq:2 * tq] = v_ref[...].astype(BF16)

    col = lax.broadcasted_iota(jnp.int32, (1, K_SUB), 1)
    unreal = jnp.where(i == 0, tq, 0)
    for r0 in range(0, tq, Q_SUB):
        neg = jnp.where(col + r0 < unreal, NEG, 0.0)
        for h in range(A_HEADS):
            hs = slice(h * A_HEAD_DIM, (h + 1) * A_HEAD_DIM)
            qh = (q_ref[r0:r0 + Q_SUB, hs] * scale).astype(BF16)
            s = _nt_dot(qh, kwin[r0:r0 + K_SUB, hs]) + bias_ref[h] + neg
            o_ref[r0:r0 + Q_SUB, hs] = _softmax_pv(s, vwin[r0:r0 + K_SUB, hs], jnp.exp2)


def band_attention_prompt(proj, bias, *, tq):
    s = proj.shape[0]
    assert tq >= A_WINDOW and tq % Q_SUB == 0 and Q_SUB % CHUNK == 0 and s % tq == 0
    return pl.pallas_call(
        _band_prompt_body,
        out_shape=jax.ShapeDtypeStruct((s, A_WIDTH), F32),
        grid=(s // tq,),
        in_specs=[
            pl.BlockSpec((tq, A_WIDTH), lambda i: (i, EVEN_QA)),
            pl.BlockSpec((tq, A_WIDTH), lambda i: (i, EVEN_KA)),
            pl.BlockSpec((tq, A_WIDTH), lambda i: (i, EVEN_VA)),
            pl.BlockSpec((A_HEADS, Q_SUB, K_SUB), lambda i: (0, 0, 0)),
        ],
        out_specs=pl.BlockSpec((tq, A_WIDTH), lambda i: (i, 0)),
        scratch_shapes=[pltpu.VMEM((2 * tq, A_WIDTH), BF16), pltpu.VMEM((2 * tq, A_WIDTH), BF16)],
        compiler_params=_params(("arbitrary",)),
        name="band_attention_prompt",
    )(proj, proj, proj, bias)


def _band_sample_body(q_ref, nk_ref, nv_ref, ck_ref, cv_ref, bc_ref, bn_ref, o_ref):
    scale = A_HEAD_DIM ** -0.5
    for h in range(A_HEADS):
        hs = slice(h * A_HEAD_DIM, (h + 1) * A_HEAD_DIM)
        qh = q_ref[:, hs].astype(BF16)
        s1 = _nt_dot(qh, ck_ref[:, h, :].astype(BF16)) * scale + bc_ref[h]
        s2 = _nt_dot(qh, nk_ref[:, hs].astype(BF16)) * scale + bn_ref[h]
        m = jnp.maximum(jnp.max(s1, axis=-1, keepdims=True), jnp.max(s2, axis=-1, keepdims=True))
        p1 = jnp.exp(s1 - m)
        p2 = jnp.exp(s2 - m)
        inv = 1.0 / (jnp.sum(p1, axis=-1, keepdims=True) + jnp.sum(p2, axis=-1, keepdims=True))
        o = _dot(p1.astype(BF16), cv_ref[:, h, :].astype(BF16)) + _dot(p2.astype(BF16), nv_ref[:, hs].astype(BF16))
        o_ref[:, hs] = o * inv


def band_attention_sample(proj, cache_k, cache_v, layer, bias_cache, bias_new, *, t):
    _, b, lc, _, _ = cache_k.shape
    cache = pl.BlockSpec((None, None, lc, A_HEADS, A_HEAD_DIM), lambda i: (layer, i, 0, 0, 0))
    return pl.pallas_call(
        _band_sample_body,
        out_shape=jax.ShapeDtypeStruct((b * t, A_WIDTH), F32),
        grid=(b,),
        in_specs=[
            pl.BlockSpec((t, A_WIDTH), lambda i: (i, EVEN_QA)),
            pl.BlockSpec((t, A_WIDTH), lambda i: (i, EVEN_KA)),
            pl.BlockSpec((t, A_WIDTH), lambda i: (i, EVEN_VA)),
            cache, cache,
            pl.BlockSpec((A_HEADS, t, lc), lambda i: (0, 0, 0)),
            pl.BlockSpec((A_HEADS, t, t), lambda i: (0, 0, 0)),
        ],
        out_specs=pl.BlockSpec((t, A_WIDTH), lambda i: (i, 0)),
        compiler_params=_params(("parallel",)),
        name="band_attention_sample",
    )(proj, proj, proj, cache_k, cache_v, bias_cache, bias_new)


def _mem_attn_body(q_ref, mk_ref, mv_ref, o_ref):
    scale = M_HEAD_DIM ** -0.5
    for h in range(M_HEADS):
        hs = slice(h * M_HEAD_DIM, (h + 1) * M_HEAD_DIM)
        s = _nt_dot(q_ref[:, hs].astype(BF16), mk_ref[:, h, :].astype(BF16)) * scale
        o_ref[:, hs] = _softmax_pv(s, mv_ref[:, h, :].astype(BF16))


def mem_attention(proj, mk, mv, kmap, vmap, *, batch, seq, tl):
    nt = seq // tl
    mem = lambda f: pl.BlockSpec((None, None, MEM_LEN, M_HEADS, M_HEAD_DIM), lambda i, j: f(i))
    return pl.pallas_call(
        _mem_attn_body,
        out_shape=jax.ShapeDtypeStruct((batch * seq, M_WIDTH), F32),
        grid=(batch, nt),
        in_specs=[pl.BlockSpec((tl, M_WIDTH), lambda i, j: (i * nt + j, QM_BLOCK)), mem(kmap), mem(vmap)],
        out_specs=pl.BlockSpec((tl, M_WIDTH), lambda i, j: (i * nt + j, 0)),
        compiler_params=_params(("parallel", "parallel")),
        name="mem_attention",
    )(proj, mk, mv)


def _shift_rows(x, halo, s):
    rolled = pltpu.roll(x, s, 0)
    halo_rolled = pltpu.roll(halo, s, 0)
    row = lax.broadcasted_iota(jnp.int32, halo.shape, 0)
    top = jnp.where(row < s, halo_rolled, rolled[0:SUBLANES])
    if x.shape[0] == SUBLANES:
        return top
    return jnp.concatenate([top, rolled[SUBLANES:]], axis=0)


def _causal_dwconv(x, halo, w_ref, b_ref):
    width = w_ref.shape[0]
    y = b_ref[...]
    for j in range(width):
        s = width - 1 - j
        xs = x if s == 0 else _shift_rows(x, halo, s)
        y = y + xs * w_ref[j:j + 1, :]
    return y


def _residual_ln(acc, rows, x_ref, g_ref, b_ref, o_ref, o16_ref, olo_ref=None):
    y = ALPHA * x_ref[rows, :] + acc
    mu = jnp.mean(y, axis=-1, keepdims=True)
    d = y - mu
    var = jnp.mean(d * d, axis=-1, keepdims=True)
    out = d * lax.rsqrt(var + EPS) * g_ref[...] + b_ref[...]
    o_ref[rows, :] = out
    hi, lo = _split_bf16(out)
    o16_ref[rows, :] = hi
    if olo_ref is not None:
        olo_ref[rows, :] = lo


def _sub_blocks(tm):
    sub = min(TAIL_SUB, tm)
    return [slice(r, r + sub) for r in range(0, tm, sub)]


def _mix_matmul(parts, w_ref):
    acc, r0 = None, 0
    for p in parts:
        term = _dot(p.astype(BF16), w_ref[r0:r0 + p.shape[1], :])
        acc = term if acc is None else acc + term
        r0 += p.shape[1]
    return acc


def _even_tail_body(gate_ref, bx_ref, bg_ref, cg_ref, ya_ref, ym_ref, cw_ref, cb_ref, halo_ref,
                    x_ref, w_ref, g_ref, b_ref, o_ref, o16_ref, olo_ref, tail_ref, carry):
    @pl.when(pl.program_id(1) == 0)
    def _():
        carry[...] = halo_ref[...]

    tm = bx_ref.shape[0]
    conv_in = lambda rows: cg_ref[rows, :] * bx_ref[rows, :]
    for rows in _sub_blocks(tm):
        halo = carry[...] if rows.start == 0 else conv_in(slice(rows.start - SUBLANES, rows.start))
        yb = bg_ref[rows, :] * _causal_dwconv(conv_in(rows), halo, cw_ref, cb_ref)
        sg = _silu(gate_ref[rows, :])
        parts = (ya_ref[rows, :] * sg[:, 0:A_WIDTH],
                 yb * sg[:, A_WIDTH:A_WIDTH + B_WIDTH],
                 ym_ref[rows, :] * sg[:, A_WIDTH + B_WIDTH:MIX_WIDTH])
        _residual_ln(_mix_matmul(parts, w_ref), rows, x_ref, g_ref, b_ref, o_ref, o16_ref, olo_ref)
    last = conv_in(slice(tm - SUBLANES, tm))
    carry[...] = last
    tail_ref[...] = last


def _layer_param(shape, layer):
    return pl.BlockSpec((None,) + shape, lambda i, j: (layer,) + (0,) * len(shape))


def _out_weight(layer):
    return pl.BlockSpec((None, MIX_WIDTH, D_MODEL), lambda i, j: (layer, 0, 0), pipeline_mode=pl.Buffered(1))


def even_tail(proj, ya, ym, x, conv_w, conv_b, halo, halo_layer, w_out, ln_g, ln_b, conv_layer, layer, *, seq, tm):
    b = halo.shape[1]
    nt = seq // tm
    row = lambda i, j: i * nt + j
    rows = lambda width, col: pl.BlockSpec((tm, width), lambda i, j: (row(i, j), col))
    return pl.pallas_call(
        _even_tail_body,
        out_shape=(jax.ShapeDtypeStruct((b * seq, D_MODEL), F32),
                   jax.ShapeDtypeStruct((b * seq, D_MODEL), BF16),
                   jax.ShapeDtypeStruct((b * seq, D_MODEL), BF16),
                   jax.ShapeDtypeStruct((b, SUBLANES, B_WIDTH), F32)),
        grid=(b, nt),
        in_specs=[
            rows(MIX_WIDTH, 0), rows(B_WIDTH, EVEN_BX), rows(B_WIDTH, EVEN_BGATE), rows(B_WIDTH, EVEN_CGATE),
            rows(A_WIDTH, 0), rows(M_WIDTH, 0),
            _layer_param((B_CONV, B_WIDTH), conv_layer), _layer_param((1, B_WIDTH), conv_layer),
            pl.BlockSpec((None, None, SUBLANES, B_WIDTH), lambda i, j: (halo_layer, i, 0, 0)),
            rows(D_MODEL, 0), _out_weight(layer), _layer_param((1, D_MODEL), layer), _layer_param((1, D_MODEL), layer),
        ],
        out_specs=(rows(D_MODEL, 0), rows(D_MODEL, 0), rows(D_MODEL, 0),
                   pl.BlockSpec((None, SUBLANES, B_WIDTH), lambda i, j: (i, 0, 0))),
        scratch_shapes=[pltpu.VMEM((SUBLANES, B_WIDTH), F32)],
        compiler_params=_params(("arbitrary", "arbitrary")),
        name="even_tail",
    )(proj, proj, proj, proj, ya, ym, conv_w, conv_b, halo, x, w_out, ln_g, ln_b)


def _odd_tail_body(gate_ref, oc_ref, hc_ref, yd_ref, ym_ref, nw_ref, x_ref, w_ref, g_ref, b_ref, o_ref, o16_ref):
    for rows in _sub_blocks(x_ref.shape[0]):
        sg = _silu(gate_ref[rows, :])
        yc = hc_ref[rows, :] * jax.nn.sigmoid(oc_ref[rows, :]) * sg[:, 0:C_WIDTH]
        z = yd_ref[rows, :] * sg[:, C_WIDTH:C_WIDTH + D_WIDTH]
        yd = z * lax.rsqrt(jnp.mean(z * z, axis=-1, keepdims=True) + EPS) * nw_ref[...]
        parts = (yc, yd, ym_ref[rows, :] * sg[:, C_WIDTH + D_WIDTH:MIX_WIDTH])
        _residual_ln(_mix_matmul(parts, w_ref), rows, x_ref, g_ref, b_ref, o_ref, o16_ref)


def odd_tail(proj, hc, yd, ym, x, norm_w, norm_layer, w_out, ln_g, ln_b, layer, *, tm):
    m = x.shape[0]
    rows = lambda width, col: pl.BlockSpec((tm, width), lambda i, j: (i, col))
    return pl.pallas_call(
        _odd_tail_body,
        out_shape=(jax.ShapeDtypeStruct((m, D_MODEL), F32), jax.ShapeDtypeStruct((m, D_MODEL), BF16)),
        grid=(m // tm, 1),
        in_specs=[
            rows(MIX_WIDTH, 0), rows(C_WIDTH, ODD_OC), rows(C_WIDTH, 0), rows(D_WIDTH, 0), rows(M_WIDTH, 0),
            _layer_param((1, D_WIDTH), norm_layer),
            rows(D_MODEL, 0), _out_weight(layer), _layer_param((1, D_MODEL), layer), _layer_param((1, D_MODEL), layer),
        ],
        out_specs=(rows(D_MODEL, 0), rows(D_MODEL, 0)),
        compiler_params=_params(("parallel", "arbitrary")),
        name="odd_tail",
    )(proj, proj, hc, yd, ym, norm_w, x, w_out, ln_g, ln_b)


def _tri_masks(t):
    ri = lax.broadcasted_iota(jnp.int32, (t, t), 0)
    ci = lax.broadcasted_iota(jnp.int32, (t, t), 1)
    return ri >= ci, ri <= ci


def _cumsum_cols(lower, x):
    return jnp.dot(lower.astype(F32), x, precision=lax.Precision.HIGHEST, preferred_element_type=F32)


def _cumsum_rows(upper, x):
    return jnp.dot(x, upper.astype(F32), precision=lax.Precision.HIGHEST, preferred_element_type=F32)


def _state_spec(shape, layer):
    return pl.BlockSpec((None, 1) + shape, lambda i, j: (layer, i) + (0,) * len(shape))


def _state_out_spec(shape):
    return pl.BlockSpec((1,) + shape, lambda i, j: (i,) + (0,) * len(shape))


def _mlstm_body(q_ref, k_ref, v_ref, gc_ref, gr_ref, c0_ref, n0_ref, m0_ref,
                h_ref, c_ref, n_ref, m_ref):
    @pl.when(pl.program_id(1) == 0)
    def _():
        c_ref[...] = c0_ref[...]
        n_ref[...] = n0_ref[...]
        m_ref[...] = m0_ref[...]

    t = q_ref.shape[0]
    scale = C_HEAD_DIM ** -0.5
    lower, upper = _tri_masks(t)
    gc = gc_ref[...]
    gr = gr_ref[...]
    ig_c = gc[:, IG_OFF:IG_OFF + C_HEADS]
    ig_r = gr[IG_OFF:IG_OFF + C_HEADS, :]
    b_c = _cumsum_cols(lower, _log_sigmoid(gc))[:, FG_OFF:FG_OFF + C_HEADS]
    b_r = _cumsum_rows(upper, _log_sigmoid(gr))[FG_OFF:FG_OFF + C_HEADS, :]
    for h in range(C_HEADS):
        hs = slice(h * C_HEAD_DIM, (h + 1) * C_HEAD_DIM)
        bc = b_c[:, h:h + 1]
        ms = m_ref[0, h:h + 1, 0:1]
        dlog = jnp.where(lower, bc - b_r[h:h + 1, :] + ig_r[h:h + 1, :], -jnp.inf)
        g = bc + ms
        m = jnp.maximum(g, jnp.max(dlog, axis=1, keepdims=True))
        qf = q_ref[:, hs]
        kf = k_ref[:, hs] * scale
        vb = v_ref[:, hs].astype(BF16)
        w = _fine_nt_dot(qf, kf) * jnp.exp(dlog - m)
        gw = jnp.exp(g - m)
        cs = c_ref[0, h]
        ns = n_ref[0, h:h + 1, :]
        w_hi, w_lo = _split_bf16(w)
        num = _dot(w_hi, vb) + _dot(w_lo, vb) + gw * _fine_dot(qf, cs)
        nq = jnp.sum(w, axis=1, keepdims=True) + gw * jnp.sum(qf * ns, axis=1, keepdims=True)
        h_ref[:, hs] = num * (1.0 / jnp.maximum(jnp.abs(nq), jnp.exp(-m)))
        bt = bc[t - 1:t, :]
        wlog = bt - bc + ig_c[:, h:h + 1]
        m_new = jnp.maximum(bt + ms, jnp.max(wlog, axis=0, keepdims=True))
        wj = jnp.exp(wlog - m_new)
        decay = jnp.exp(bt + ms - m_new)
        kw = kf * wj
        kw_hi, kw_lo = _split_bf16(kw)
        c_ref[0, h] = decay * cs + _tn_dot(kw_hi, vb) + _tn_dot(kw_lo, vb)
        n_ref[0, h:h + 1, :] = decay * ns + jnp.sum(kw, axis=0, keepdims=True)
        m_ref[0, h:h + 1, :] = jnp.broadcast_to(m_new, (1, C_HEAD_DIM))


def mlstm(proj, gates_c, gates_r, c0, n0, m0, layer, *, batch, seq, t):
    nb = seq // t
    row = lambda i, j: i * nb + j
    rows = lambda width, col: pl.BlockSpec((t, width), lambda i, j: (row(i, j), col))
    shapes = (c0.shape[2:], n0.shape[2:], m0.shape[2:])
    return pl.pallas_call(
        _mlstm_body,
        out_shape=(jax.ShapeDtypeStruct((batch * seq, C_WIDTH), F32),)
        + tuple(jax.ShapeDtypeStruct((batch,) + s, F32) for s in shapes),
        grid=(batch, nb),
        in_specs=[
            rows(C_WIDTH, ODD_QC), rows(C_WIDTH, ODD_KC), rows(C_WIDTH, ODD_VC), rows(GATE_COLS, 0),
            pl.BlockSpec((None, N_GATES, t), lambda i, j: (row(i, j), 0, 0)),
        ] + [_state_spec(s, layer) for s in shapes],
        out_specs=(rows(C_WIDTH, 0),) + tuple(_state_out_spec(s) for s in shapes),
        compiler_params=_params(("arbitrary", "arbitrary")),
        name="mlstm",
    )(proj, proj, proj, gates_c, gates_r, c0, n0, m0)


def _ssd_body(xd_ref, bc_ref, gc_ref, gr_ref, wx_ref, wbc_ref, bx_ref, bbc_ref,
              alr_ref, alc_ref, dskip_ref, hx0_ref, hbc0_ref, s0_ref,
              y_ref, s_ref, hxo_ref, hbco_ref, hx, hbc):
    @pl.when(pl.program_id(1) == 0)
    def _():
        s_ref[...] = s0_ref[...]
        hx[...] = hx0_ref[0]
        hbc[...] = hbc0_ref[0]

    t = xd_ref.shape[0]
    xr = xd_ref[...]
    br = bc_ref[...]
    xs = _silu(_causal_dwconv(xr, hx[...], wx_ref, bx_ref))
    bcs = _silu(_causal_dwconv(br, hbc[...], wbc_ref, bbc_ref))
    hx[...] = xr[t - SUBLANES:t]
    hbc[...] = br[t - SUBLANES:t]
    hxo_ref[0] = xr[t - SUBLANES:t]
    hbco_ref[0] = br[t - SUBLANES:t]

    lower, upper = _tri_masks(t)
    dt_c = _softplus(gc_ref[:, DT_OFF:DT_OFF + D_HEADS])
    dt_r = _softplus(gr_ref[DT_OFF:DT_OFF + D_HEADS, :])
    a_c = _cumsum_cols(lower, dt_c * -jnp.exp(alr_ref[...]))
    a_r = _cumsum_rows(upper, dt_r * -jnp.exp(alc_ref[...]))
    for g in range(D_GROUPS):
        bmg = bcs[:, g * D_STATE:(g + 1) * D_STATE].astype(BF16)
        cmg = bcs[:, (D_GROUPS + g) * D_STATE:(D_GROUPS + g + 1) * D_STATE].astype(BF16)
        cb = _nt_dot(cmg, bmg)
        for hh in range(D_HG):
            h = g * D_HG + hh
            hs = slice(h * D_HEAD_DIM, (h + 1) * D_HEAD_DIM)
            ac = a_c[:, h:h + 1]
            dtc = dt_c[:, h:h + 1]
            lw = jnp.exp(jnp.where(lower, ac - a_r[h:h + 1, :], -jnp.inf))
            xh = xs[:, hs]
            sh = s_ref[0, h]
            y = _dot((cb * (lw * dt_r[h:h + 1, :])).astype(BF16), xh.astype(BF16))
            y = y + jnp.exp(ac) * _nt_dot(cmg, sh.astype(BF16))
            y_ref[:, hs] = y + dskip_ref[:, hs] * xh
            at = ac[t - 1:t, :]
            wj = jnp.exp(at - ac) * dtc
            s_ref[0, h] = jnp.exp(at) * sh + _tn_dot((xh * wj).astype(BF16), bmg)


def ssd(proj, gates_c, gates_r, wx, wbc, bx, bbc, alog_r, alog_c, dskip, hx0, hbc0, s0, layer, state_layer,
        *, batch, seq, t):
    nb = seq // t
    row = lambda i, j: i * nb + j
    rows = lambda width, col: pl.BlockSpec((t, width), lambda i, j: (row(i, j), col))
    param = lambda a: _layer_param(a.shape[1:], layer)
    shapes = (s0.shape[2:], hx0.shape[2:], hbc0.shape[2:])
    return pl.pallas_call(
        _ssd_body,
        out_shape=(jax.ShapeDtypeStruct((batch * seq, D_WIDTH), F32),)
        + tuple(jax.ShapeDtypeStruct((batch,) + s, F32) for s in shapes),
        grid=(batch, nb),
        in_specs=[
            rows(D_WIDTH, ODD_XD), rows(D_BC, ODD_BC_BLOCK),
            rows(GATE_COLS, 0), pl.BlockSpec((None, N_GATES, t), lambda i, j: (row(i, j), 0, 0)),
            param(wx), param(wbc), param(bx), param(bbc), param(alog_r), param(alog_c), param(dskip),
            _state_spec(shapes[1], state_layer), _state_spec(shapes[2], state_layer), _state_spec(shapes[0], state_layer),
        ],
        out_specs=(rows(D_WIDTH, 0),) + tuple(_state_out_spec(s) for s in shapes),
        scratch_shapes=[pltpu.VMEM((SUBLANES, D_WIDTH), F32), pltpu.VMEM((SUBLANES, D_BC), F32)],
        compiler_params=_params(("arbitrary", "arbitrary")),
        name="ssd",
    )(proj, proj, gates_c, gates_r, wx, wbc, bx, bbc, alog_r, alog_c, dskip, hx0, hbc0, s0)


def _pad_state_rows(buf):
    pad = jnp.zeros(buf.shape[:-2] + (SUBLANES - buf.shape[-2], buf.shape[-1]), F32)
    return jnp.concatenate([pad, buf], axis=-2)


def _row_gates(gates_c, t):
    m = gates_c.shape[0]
    return gates_c[:, :N_GATES].reshape(m // t, t, N_GATES).transpose(0, 2, 1)


def _even_weights(w_in, b_in):
    off = A_WIDTH * 3 + B_WIDTH * 3
    order = lambda a: jnp.concatenate([a[..., off + M_WIDTH:], a[..., off:off + M_WIDTH], a[..., :off]], axis=-1)
    return order(w_in).astype(BF16), order(b_in)[:, None, :]


def _odd_weights(w_in, b_in):
    c4 = 4 * C_WIDTH
    g0 = c4 + 2 * C_HEADS
    d0 = g0 + D_XBC
    q0 = d0 + D_HEADS
    t0 = q0 + M_WIDTH
    big = lambda a: jnp.concatenate([a[..., t0:], a[..., q0:t0], a[..., :c4], a[..., g0:d0]], axis=-1)
    small = lambda a: jnp.concatenate(
        [a[..., c4:g0], a[..., d0:q0], jnp.zeros(a.shape[:-1] + (GATE_COLS - N_GATES,), a.dtype)], axis=-1)
    qk_lo = _split_bf16(w_in[..., :2 * C_WIDTH])[1]
    small_hi, small_lo = _split_bf16(small(w_in))
    return (big(w_in).astype(BF16), big(b_in)[:, None, :], qk_lo), (small_hi, small(b_in)[:, None, :], small_lo)


def _heads(a, batch, seq, col, last):
    part = a.reshape(batch, seq, a.shape[1])[:, seq - last:, col * A_WIDTH:(col + 1) * A_WIDTH]
    return part.reshape(batch, last, A_HEADS, A_HEAD_DIM)


def kernel(x_prompt, x_sample, mem_prompt, cache_a_k, cache_a_v, cache_mem_k, cache_mem_v, state_b_conv, state_c_C, state_c_n, state_c_m, state_d_ssm, state_d_conv, w_in_even, b_in_even, w_in_odd, b_in_odd, w_out, w_mem_kv, ln_g, ln_b, rel_bias_table, b_conv_w, b_conv_b, d_conv_w, d_conv_b, d_A_log, d_skip, d_norm_w):
    bp, sp, _ = x_prompt.shape
    bs, ss, _ = x_sample.shape
    n_even, n_odd = w_in_even.shape[0], w_in_odd.shape[0]
    lc = cache_a_k.shape[2]
    xp = x_prompt.reshape(bp * sp, D_MODEL)
    xs = x_sample.reshape(bs * ss, D_MODEL)
    xp16, xs16 = xp.astype(BF16), xs.astype(BF16)
    mem = mem_prompt.reshape(bp * MEM_LEN, D_MODEL).astype(BF16)

    we, be = _even_weights(w_in_even, b_in_even)
    (wo_big, bo_big, wo_qk_lo), (wo_small, bo_small, wo_small_lo) = _odd_weights(w_in_odd, b_in_odd)
    qk_blocks = (ODD_QC * C_WIDTH // 512, (ODD_KC + 1) * C_WIDTH // 512)
    w_out_b = w_out.astype(BF16)
    w_mem_b = w_mem_kv.astype(BF16)
    zero_bias = jnp.zeros((1, 1, 2 * M_WIDTH), F32)
    ln_g3, ln_b3 = ln_g[:, None, :], ln_b[:, None, :]
    conv_b3 = b_conv_b[:, None, :]
    norm_w3 = d_norm_w[:, None, :]
    d_wx, d_wbc = d_conv_w[:, :, :D_WIDTH], d_conv_w[:, :, D_WIDTH:]
    d_bx, d_bbc = d_conv_b[:, None, :D_WIDTH], d_conv_b[:, None, D_WIDTH:]
    alog_r, alog_c = d_A_log[:, None, :], d_A_log[:, :, None]
    dskip = jnp.repeat(d_skip, D_HEAD_DIM, axis=-1)[:, None, :]

    s_halo_b = _pad_state_rows(state_b_conv)
    s_halo_d = _pad_state_rows(state_d_conv)
    s_hx0, s_hbc0 = s_halo_d[..., :D_WIDTH], s_halo_d[..., D_WIDTH:]
    s_m0 = jnp.broadcast_to(state_c_m[..., None], state_c_m.shape + (C_HEAD_DIM,))
    zeros = lambda *shape: jnp.zeros((1, bp) + shape, F32)
    p_halo_b, p_hx0, p_hbc0 = zeros(SUBLANES, B_WIDTH), zeros(SUBLANES, D_WIDTH), zeros(SUBLANES, D_BC)
    p_c0, p_n0, p_m0 = zeros(C_HEADS, C_HEAD_DIM, C_HEAD_DIM), zeros(C_HEADS, C_HEAD_DIM), zeros(C_HEADS, C_HEAD_DIM)
    p_s0 = zeros(D_HEADS, D_HEAD_DIM, D_STATE)

    outs = {k: [] for k in ("p_mk", "p_mv", "p_ak", "p_av", "p_bc", "p_cc", "p_cn", "p_cm", "p_ds", "p_dc",
                            "s_ak", "s_av", "s_bc", "s_cc", "s_cn", "s_cm", "s_ds", "s_dc")}
    tp = min(SCAN_BLOCK, sp)
    ts = min(SCAN_BLOCK, ss)
    for l in range(DEPTH):
        i = l // 2
        mkv = linear(mem, w_mem_b, zero_bias.repeat(DEPTH, 0), l)
        mkv5 = mkv.reshape(1, bp, MEM_LEN, 2 * M_HEADS, M_HEAD_DIM)
        mk5, mv5 = mkv5[:, :, :, :M_HEADS], mkv5[:, :, :, M_HEADS:]
        outs["p_mk"].append(mk5[0])
        outs["p_mv"].append(mv5[0])
        p_mem = (mk5, mv5, lambda b: (0, b, 0, 0, 0), lambda b: (0, b, 0, 0, 0))
        s_mem = (cache_mem_k, cache_mem_v, lambda b, l=l: (l, b, 0, 0, 0), lambda b, l=l: (l, b, 0, 0, 0))
        if l % 2 == 0:
            proj_p = linear(xp16, we, be, i)
            proj_s = linear(xs16, we, be, i)
            ya_p = band_attention_prompt(proj_p, rel_bias_band(rel_bias_table[i]), tq=A_WINDOW)
            bias_s = rel_bias(rel_bias_table[i], ss, lc + ss, lc)
            ya_s = band_attention_sample(proj_s, cache_a_k, cache_a_v, i, bias_s[:, :, :lc], bias_s[:, :, lc:], t=ss)
            ym_p = mem_attention(proj_p, *p_mem, batch=bp, seq=sp, tl=min(sp, 512))
            ym_s = mem_attention(proj_s, *s_mem, batch=bs, seq=ss, tl=min(ss, 512))
            tail = (b_conv_w, conv_b3)
            xp_new, xp16_new, xplo_new, tail_p = even_tail(proj_p, ya_p, ym_p, xp, *tail, p_halo_b, 0, w_out_b, ln_g3, ln_b3,
                                                 i, l, seq=sp, tm=min(sp, 256))
            xs_new, xs16_new, xslo_new, tail_s = even_tail(proj_s, ya_s, ym_s, xs, *tail, s_halo_b, i, w_out_b, ln_g3, ln_b3,
                                                 i, l, seq=ss, tm=min(ss, 256))
            outs["p_ak"].append(_heads(proj_p, bp, sp, EVEN_KA, A_WINDOW))
            outs["p_av"].append(_heads(proj_p, bp, sp, EVEN_VA, A_WINDOW))
            outs["s_ak"].append(_heads(proj_s, bs, ss, EVEN_KA, ss))
            outs["s_av"].append(_heads(proj_s, bs, ss, EVEN_VA, ss))
            outs["p_bc"].append(tail_p[:, SUBLANES - (B_CONV - 1):])
            outs["s_bc"].append(tail_s[:, SUBLANES - (B_CONV - 1):])
        else:
            ssd_params = (d_wx, d_wbc, d_bx, d_bbc, alog_r, alog_c, dskip)
            res = []
            for x, x16, xlo, batch, seq, t, mem_args, states, sl in (
                    (xp, xp16, xplo, bp, sp, tp, p_mem, (p_c0, p_n0, p_m0, p_hx0, p_hbc0, p_s0), 0),
                    (xs, xs16, xslo, bs, ss, ts, s_mem, (state_c_C, state_c_n, s_m0, s_hx0, s_hbc0, state_d_ssm), i)):
                c0, n0, m0, hx0, hbc0, s0 = states
                proj = linear(x16, wo_big, bo_big, i, fine=(xlo, wo_qk_lo, *qk_blocks) if i == 0 else None)
                gates_c = linear(x16, wo_small, bo_small, i, tn=GATE_COLS, fine=(xlo, wo_small_lo, 0, 1))
                gates_r = _row_gates(gates_c, t)
                hc, cc, cn, cm = mlstm(proj, gates_c, gates_r, c0, n0, m0, sl, batch=batch, seq=seq, t=t)
                yd, s_new, hx, hbc = ssd(proj, gates_c, gates_r, *ssd_params, hx0, hbc0, s0, i, sl,
                                         batch=batch, seq=seq, t=t)
                ym = mem_attention(proj, *mem_args, batch=batch, seq=seq, tl=min(seq, 512))
                x_new, x16_new = odd_tail(proj, hc, yd, ym, x, norm_w3, i, w_out_b, ln_g3, ln_b3, l, tm=256)
                dbuf = jnp.concatenate([hx, hbc], axis=-1)[:, SUBLANES - (D_CONV - 1):]
                res.append((x_new, x16_new, cc, cn, cm[:, :, 0], s_new, dbuf))
            (xp_new, xp16_new, *st_p), (xs_new, xs16_new, *st_s) = res
            for key, val in zip(("p_cc", "p_cn", "p_cm", "p_ds", "p_dc"), st_p):
                outs[key].append(val)
            for key, val in zip(("s_cc", "s_cn", "s_cm", "s_ds", "s_dc"), st_s):
                outs[key].append(val)
        xp, xs, xp16, xs16 = xp_new, xs_new, xp16_new, xs16_new
        if l % 2 == 0:
            xplo, xslo = xplo_new, xslo_new
    st = jnp.stack
    return (xp.reshape(bp, sp, D_MODEL), xs.reshape(bs, ss, D_MODEL),
            st(outs["p_mk"]), st(outs["p_mv"]), st(outs["p_ak"]), st(outs["p_av"]), st(outs["p_bc"]),
            st(outs["p_cc"]), st(outs["p_cn"]), st(outs["p_cm"]), st(outs["p_ds"]), st(outs["p_dc"]),
            st(outs["s_ak"]), st(outs["s_av"]), st(outs["s_bc"]),
            st(outs["s_cc"]), st(outs["s_cn"]), st(outs["s_cm"]), st(outs["s_ds"]), st(outs["s_dc"]))
```

```python
import functools

import jax
import jax.numpy as jnp
from jax import lax
from jax.experimental import pallas as pl
from jax.experimental.pallas import tpu as pltpu

F32 = jnp.float32
BF16 = jnp.bfloat16

D_MODEL = 2048
DEPTH = 4
CHUNK = 64
ALPHA = (2 * DEPTH) ** 0.25
EPS = 1e-5
NEG = -1e30

A_WIDTH = D_MODEL // 2
A_HEAD_DIM = 128
A_HEADS = A_WIDTH // A_HEAD_DIM
A_BAND = 8
A_WINDOW = A_BAND * CHUNK
REL_CLIP = 128
B_WIDTH = D_MODEL // 2
B_CONV = 3
C_WIDTH = D_MODEL // 2
C_HEAD_DIM = 128
C_HEADS = C_WIDTH // C_HEAD_DIM
D_WIDTH = D_MODEL // 2
D_HEAD_DIM = 64
D_HEADS = D_WIDTH // D_HEAD_DIM
D_GROUPS = 2
D_HG = D_HEADS // D_GROUPS
D_STATE = 128
D_CONV = 4
D_BC = 2 * D_GROUPS * D_STATE
D_XBC = D_WIDTH + D_BC
MEM_LEN = 256
M_WIDTH = D_MODEL // 4
M_HEADS = 4
M_HEAD_DIM = M_WIDTH // M_HEADS
MIX_WIDTH = A_WIDTH + B_WIDTH + M_WIDTH

LANES = 128
SUBLANES = 8
VMEM_LIMIT_BYTES = 56 * 1024 * 1024

EVEN_QA, EVEN_KA, EVEN_VA, EVEN_BX, EVEN_BGATE, EVEN_CGATE = 3, 4, 5, 6, 7, 8
ODD_QC, ODD_KC, ODD_VC, ODD_OC, ODD_XD = 3, 4, 5, 6, 7
QM_BLOCK = MIX_WIDTH // M_WIDTH
ODD_BC_BLOCK = (MIX_WIDTH + M_WIDTH + 5 * 1024) // D_BC
GATE_COLS = LANES
IG_OFF, FG_OFF, DT_OFF = 0, C_HEADS, 2 * C_HEADS
N_GATES = 2 * C_HEADS + D_HEADS

SCAN_BLOCK = 256
Q_SUB = 256
K_SUB = Q_SUB + A_WINDOW
BAND_ROWS = (A_BAND + 1) * CHUNK
LOG2E = 1.4426950408889634
LINEAR_ROWS = 2048
TAIL_SUB = 128


def _params(semantics):
    return pltpu.CompilerParams(dimension_semantics=semantics, vmem_limit_bytes=VMEM_LIMIT_BYTES)


def _nt_dot(a, b):
    return lax.dot_general(a, b, (((1,), (1,)), ((), ())), preferred_element_type=F32)


def _tn_dot(a, b):
    return lax.dot_general(a, b, (((0,), (0,)), ((), ())), preferred_element_type=F32)


def _dot(a, b):
    return jnp.dot(a, b, preferred_element_type=F32)


def _fine_dot(a, b):
    return jnp.dot(a, b, precision=lax.Precision.HIGHEST, preferred_element_type=F32)


def _fine_nt_dot(a, b):
    return lax.dot_general(a, b, (((1,), (1,)), ((), ())), precision=lax.Precision.HIGHEST,
                           preferred_element_type=F32)


def _silu(x):
    return x * jax.nn.sigmoid(x)


def _softplus(x):
    return jnp.maximum(x, 0.0) + jnp.log1p(jnp.exp(-jnp.abs(x)))


def _log_sigmoid(x):
    return -_softplus(-x)


def _softmax_pv(s, v, exp=jnp.exp):
    m = jnp.max(s, axis=-1, keepdims=True)
    p = exp(s - m)
    inv = 1.0 / jnp.sum(p, axis=-1, keepdims=True)
    return _dot(p.astype(BF16), v) * inv


def _split_bf16(a):
    hi = a.astype(BF16)
    return hi, (a - hi.astype(F32)).astype(BF16)


def _linear_body(x_ref, w_ref, b_ref, o_ref):
    o_ref[...] = _dot(x_ref[...], w_ref[...]) + b_ref[...]


def _linear_fine_body(x_ref, w_ref, b_ref, xl_ref, wl_ref, o_ref, *, fine):
    j = pl.program_id(1)
    o_ref[...] = _dot(x_ref[...], w_ref[...]) + b_ref[...]

    @pl.when((j >= fine[0]) & (j < fine[1]))
    def _():
        o_ref[...] += _dot(xl_ref[...], w_ref[...]) + _dot(x_ref[...], wl_ref[...])


def linear(x, w, b, layer, *, tn=512, fine=None):
    m, k = x.shape
    n = w.shape[2]
    tm = min(m, LINEAR_ROWS)
    assert m % tm == 0 and n % tn == 0, (m, n, tm, tn)
    in_specs = [
        pl.BlockSpec((tm, k), lambda i, j: (i, 0)),
        pl.BlockSpec((None, k, tn), lambda i, j: (layer, 0, j)),
        pl.BlockSpec((None, 1, tn), lambda i, j: (layer, 0, j)),
    ]
    body, args = _linear_body, (x, w, b)
    if fine is not None:
        x_lo, w_lo, j0, j1 = fine
        assert w_lo.shape[2] == (j1 - j0) * tn
        in_specs += [
            pl.BlockSpec((tm, k), lambda i, j: (i, 0)),
            pl.BlockSpec((None, k, tn), lambda i, j: (layer, 0, jnp.clip(j - j0, 0, j1 - j0 - 1))),
        ]
        body, args = functools.partial(_linear_fine_body, fine=(j0, j1)), (x, w, b, x_lo, w_lo)
    return pl.pallas_call(
        body,
        out_shape=jax.ShapeDtypeStruct((m, n), F32),
        grid=(m // tm, n // tn),
        in_specs=in_specs,
        out_specs=pl.BlockSpec((tm, tn), lambda i, j: (i, j)),
        compiler_params=_params(("parallel", "arbitrary")),
        name="linear",
    )(*args)


def _table_select(tab_ref, h, idx):
    def step(t, acc):
        return jnp.where(idx == t, tab_ref[h, t], acc)

    return lax.fori_loop(0, 2 * REL_CLIP + 1, step, jnp.zeros(idx.shape, F32))


def _relbias_body(tab_ref, o_ref, *, offset):
    ql, kl = o_ref.shape[1], o_ref.shape[2]
    qi = lax.broadcasted_iota(jnp.int32, (ql, kl), 0)
    ki = lax.broadcasted_iota(jnp.int32, (ql, kl), 1)
    idx = jnp.clip(qi + offset - ki, -REL_CLIP, REL_CLIP) + REL_CLIP
    o_ref[0] = _table_select(tab_ref, pl.program_id(0), idx)


def rel_bias(table, q_len, k_len, offset):
    heads = table.shape[0]
    return pl.pallas_call(
        functools.partial(_relbias_body, offset=offset),
        out_shape=jax.ShapeDtypeStruct((heads, q_len, k_len), F32),
        grid=(heads,),
        in_specs=[pl.BlockSpec(memory_space=pltpu.SMEM)],
        out_specs=pl.BlockSpec((1, q_len, k_len), lambda h: (h, 0, 0)),
        compiler_params=_params(("arbitrary",)),
        name="rel_bias",
    )(table)


def _relbias_band_body(tab_ref, o_ref):
    period = 2 * A_WINDOW
    u = lax.broadcasted_iota(jnp.int32, (SUBLANES, period), 1)
    d = jnp.where(u < K_SUB, u, u - period)
    idx = jnp.clip(A_WINDOW - d, -REL_CLIP, REL_CLIP) + REL_CLIP
    g = _table_select(tab_ref, pl.program_id(0), idx)
    full = pltpu.roll(jnp.broadcast_to(g[0:1], (Q_SUB, period)), 0, 1, stride=1, stride_axis=0)
    r = lax.broadcasted_iota(jnp.int32, (Q_SUB, K_SUB), 0)
    w = lax.broadcasted_iota(jnp.int32, (Q_SUB, K_SUB), 1)
    first = (r // CHUNK) * CHUNK
    in_band = (w >= first) & (w < first + BAND_ROWS)
    o_ref[0] = jnp.where(in_band, full[:, :K_SUB] * LOG2E, NEG)


def rel_bias_band(table):
    heads = table.shape[0]
    return pl.pallas_call(
        _relbias_band_body,
        out_shape=jax.ShapeDtypeStruct((heads, Q_SUB, K_SUB), F32),
        grid=(heads,),
        in_specs=[pl.BlockSpec(memory_space=pltpu.SMEM)],
        out_specs=pl.BlockSpec((1, Q_SUB, K_SUB), lambda h: (h, 0, 0)),
        compiler_params=_params(("arbitrary",)),
        name="rel_bias_band",
    )(table)


def _band_prompt_body(q_ref, k_ref, v_ref, bias_ref, o_ref, kwin, vwin):
    i = pl.program_id(0)
    tq = q_ref.shape[0]
    scale = A_HEAD_DIM ** -0.5 * LOG2E

    @pl.when(i == 0)
    def _():
        kwin[0:tq] = jnp.zeros((tq, A_WIDTH), BF16)
        vwin[0:tq] = jnp.zeros((tq, A_WIDTH), BF16)

    @pl.when(i > 0)
    def _():
        kwin[0:tq] = kwin[tq:2 * tq]
        vwin[0:tq] = vwin[tq:2 * tq]

    kwin[tq:2 * tq] = k_ref[...].astype(BF16)
    vwin[tq:2 * tq] = v_ref[...].astype(BF16)

    col = lax.broadcasted_iota(jnp.int32, (1, K_SUB), 1)
    unreal = jnp.where(i == 0, tq, 0)
    for r0 in range(0, tq, Q_SUB):
        neg = jnp.where(col + r0 < unreal, NEG, 0.0)
        for h in range(A_HEADS):
            hs = slice(h * A_HEAD_DIM, (h + 1) * A_HEAD_DIM)
            qh = (q_ref[r0:r0 + Q_SUB, hs] * scale).astype(BF16)
            s = _nt_dot(qh, kwin[r0:r0 + K_SUB, hs]) + bias_ref[h] + neg
            o_ref[r0:r0 + Q_SUB, hs] = _softmax_pv(s, vwin[r0:r0 + K_SUB, hs], jnp.exp2)


def band_attention_prompt(proj, bias, *, tq):
    s = proj.shape[0]
    assert tq >= A_WINDOW and tq % Q_SUB == 0 and Q_SUB % CHUNK == 0 and s % tq == 0
    return pl.pallas_call(
        _band_prompt_body,
        out_shape=jax.ShapeDtypeStruct((s, A_WIDTH), F32),
        grid=(s // tq,),
        in_specs=[
            pl.BlockSpec((tq, A_WIDTH), lambda i: (i, EVEN_QA)),
            pl.BlockSpec((tq, A_WIDTH), lambda i: (i, EVEN_KA)),
            pl.BlockSpec((tq, A_WIDTH), lambda i: (i, EVEN_VA)),
            pl.BlockSpec((A_HEADS, Q_SUB, K_SUB), lambda i: (0, 0, 0)),
        ],
        out_specs=pl.BlockSpec((tq, A_WIDTH), lambda i: (i, 0)),
        scratch_shapes=[pltpu.VMEM((2 * tq, A_WIDTH), BF16), pltpu.VMEM((2 * tq, A_WIDTH), BF16)],
        compiler_params=_params(("arbitrary",)),
        name="band_attention_prompt",
    )(proj, proj, proj, bias)


def _head_rows(ref, h, heads):
    return ref[pl.ds(h, ref.shape[0] // heads, stride=heads), :]


def _band_sample_body(q_ref, nk_ref, nv_ref, ck_ref, cv_ref, bc_ref, bn_ref, o_ref):
    scale = A_HEAD_DIM ** -0.5
    for h in range(A_HEADS):
        hs = slice(h * A_HEAD_DIM, (h + 1) * A_HEAD_DIM)
        qh = q_ref[:, hs].astype(BF16)
        s1 = _nt_dot(qh, _head_rows(ck_ref, h, A_HEADS).astype(BF16)) * scale + bc_ref[h]
        s2 = _nt_dot(qh, nk_ref[:, hs].astype(BF16)) * scale + bn_ref[h]
        m = jnp.maximum(jnp.max(s1, axis=-1, keepdims=True), jnp.max(s2, axis=-1, keepdims=True))
        p1 = jnp.exp(s1 - m)
        p2 = jnp.exp(s2 - m)
        inv = 1.0 / (jnp.sum(p1, axis=-1, keepdims=True) + jnp.sum(p2, axis=-1, keepdims=True))
        o = _dot(p1.astype(BF16), _head_rows(cv_ref, h, A_HEADS).astype(BF16))
        o = o + _dot(p2.astype(BF16), nv_ref[:, hs].astype(BF16))
        o_ref[:, hs] = o * inv


def band_attention_sample(proj, cache_k, cache_v, layer, bias_cache, bias_new, *, t):
    _, b, rows, _ = cache_k.shape
    lc = rows // A_HEADS
    cache = pl.BlockSpec((None, None, rows, A_HEAD_DIM), lambda i: (layer, i, 0, 0))
    return pl.pallas_call(
        _band_sample_body,
        out_shape=jax.ShapeDtypeStruct((b * t, A_WIDTH), F32),
        grid=(b,),
        in_specs=[
            pl.BlockSpec((t, A_WIDTH), lambda i: (i, EVEN_QA)),
            pl.BlockSpec((t, A_WIDTH), lambda i: (i, EVEN_KA)),
            pl.BlockSpec((t, A_WIDTH), lambda i: (i, EVEN_VA)),
            cache, cache,
            pl.BlockSpec((A_HEADS, t, lc), lambda i: (0, 0, 0)),
            pl.BlockSpec((A_HEADS, t, t), lambda i: (0, 0, 0)),
        ],
        out_specs=pl.BlockSpec((t, A_WIDTH), lambda i: (i, 0)),
        compiler_params=_params(("parallel",)),
        name="band_attention_sample",
    )(proj, proj, proj, cache_k, cache_v, bias_cache, bias_new)


def _mem_attn_body(q_ref, mk_ref, mv_ref, o_ref):
    scale = M_HEAD_DIM ** -0.5
    for h in range(M_HEADS):
        hs = slice(h * M_HEAD_DIM, (h + 1) * M_HEAD_DIM)
        s = _nt_dot(q_ref[:, hs].astype(BF16), _head_rows(mk_ref, h, M_HEADS).astype(BF16)) * scale
        o_ref[:, hs] = _softmax_pv(s, _head_rows(mv_ref, h, M_HEADS).astype(BF16))


def mem_attention(proj, mk, mv, kmap, vmap, *, batch, seq, tl):
    nt = seq // tl
    mem = lambda f: pl.BlockSpec((None, None, MEM_LEN * M_HEADS, M_HEAD_DIM), lambda i, j: f(i))
    return pl.pallas_call(
        _mem_attn_body,
        out_shape=jax.ShapeDtypeStruct((batch * seq, M_WIDTH), F32),
        grid=(batch, nt),
        in_specs=[pl.BlockSpec((tl, M_WIDTH), lambda i, j: (i * nt + j, QM_BLOCK)), mem(kmap), mem(vmap)],
        out_specs=pl.BlockSpec((tl, M_WIDTH), lambda i, j: (i * nt + j, 0)),
        compiler_params=_params(("parallel", "parallel")),
        name="mem_attention",
    )(proj, mk, mv)


def _shift_rows(x, halo, s):
    rolled = pltpu.roll(x, s, 0)
    halo_rolled = pltpu.roll(halo, s, 0)
    row = lax.broadcasted_iota(jnp.int32, halo.shape, 0)
    top = jnp.where(row < s, halo_rolled, rolled[0:SUBLANES])
    if x.shape[0] == SUBLANES:
        return top
    return jnp.concatenate([top, rolled[SUBLANES:]], axis=0)


def _causal_dwconv(x, halo, w_ref, b_ref):
    width = w_ref.shape[0]
    y = b_ref[...]
    for j in range(width):
        s = width - 1 - j
        xs = x if s == 0 else _shift_rows(x, halo, s)
        y = y + xs * w_ref[j:j + 1, :]
    return y


def _residual_ln(acc, rows, x_ref, g_ref, b_ref, o_ref, o16_ref, olo_ref=None):
    y = ALPHA * x_ref[rows, :] + acc
    mu = jnp.mean(y, axis=-1, keepdims=True)
    d = y - mu
    var = jnp.mean(d * d, axis=-1, keepdims=True)
    out = d * lax.rsqrt(var + EPS) * g_ref[...] + b_ref[...]
    o_ref[rows, :] = out
    hi, lo = _split_bf16(out)
    o16_ref[rows, :] = hi
    if olo_ref is not None:
        olo_ref[rows, :] = lo


def _sub_blocks(tm):
    sub = min(TAIL_SUB, tm)
    return [slice(r, r + sub) for r in range(0, tm, sub)]


def _mix_matmul(parts, w_ref):
    acc, r0 = None, 0
    for p in parts:
        term = _dot(p.astype(BF16), w_ref[r0:r0 + p.shape[1], :])
        acc = term if acc is None else acc + term
        r0 += p.shape[1]
    return acc


def _even_tail_body(gate_ref, bx_ref, bg_ref, cg_ref, ya_ref, ym_ref, cw_ref, cb_ref, halo_ref,
                    x_ref, w_ref, g_ref, b_ref, o_ref, o16_ref, olo_ref, tail_ref, carry):
    @pl.when(pl.program_id(1) == 0)
    def _():
        carry[...] = halo_ref[...]

    tm = bx_ref.shape[0]
    conv_in = lambda rows: cg_ref[rows, :] * bx_ref[rows, :]
    for rows in _sub_blocks(tm):
        halo = carry[...] if rows.start == 0 else conv_in(slice(rows.start - SUBLANES, rows.start))
        yb = bg_ref[rows, :] * _causal_dwconv(conv_in(rows), halo, cw_ref, cb_ref)
        sg = _silu(gate_ref[rows, :])
        parts = (ya_ref[rows, :] * sg[:, 0:A_WIDTH],
                 yb * sg[:, A_WIDTH:A_WIDTH + B_WIDTH],
                 ym_ref[rows, :] * sg[:, A_WIDTH + B_WIDTH:MIX_WIDTH])
        _residual_ln(_mix_matmul(parts, w_ref), rows, x_ref, g_ref, b_ref, o_ref, o16_ref, olo_ref)
    last = conv_in(slice(tm - SUBLANES, tm))
    carry[...] = last
    tail_ref[...] = last


def _layer_param(shape, layer):
    return pl.BlockSpec((None,) + shape, lambda i, j: (layer,) + (0,) * len(shape))


def _out_weight(layer):
    return pl.BlockSpec((None, MIX_WIDTH, D_MODEL), lambda i, j: (layer, 0, 0), pipeline_mode=pl.Buffered(1))


def even_tail(proj, ya, ym, x, conv_w, conv_b, halo, halo_layer, w_out, ln_g, ln_b, conv_layer, layer, *, seq, tm):
    b = halo.shape[1]
    nt = seq // tm
    row = lambda i, j: i * nt + j
    rows = lambda width, col: pl.BlockSpec((tm, width), lambda i, j: (row(i, j), col))
    return pl.pallas_call(
        _even_tail_body,
        out_shape=(jax.ShapeDtypeStruct((b * seq, D_MODEL), F32),
                   jax.ShapeDtypeStruct((b * seq, D_MODEL), BF16),
                   jax.ShapeDtypeStruct((b * seq, D_MODEL), BF16),
                   jax.ShapeDtypeStruct((b, SUBLANES, B_WIDTH), F32)),
        grid=(b, nt),
        in_specs=[
            rows(MIX_WIDTH, 0), rows(B_WIDTH, EVEN_BX), rows(B_WIDTH, EVEN_BGATE), rows(B_WIDTH, EVEN_CGATE),
            rows(A_WIDTH, 0), rows(M_WIDTH, 0),
            _layer_param((B_CONV, B_WIDTH), conv_layer), _layer_param((1, B_WIDTH), conv_layer),
            pl.BlockSpec((None, None, SUBLANES, B_WIDTH), lambda i, j: (halo_layer, i, 0, 0)),
            rows(D_MODEL, 0), _out_weight(layer), _layer_param((1, D_MODEL), layer), _layer_param((1, D_MODEL), layer),
        ],
        out_specs=(rows(D_MODEL, 0), rows(D_MODEL, 0), rows(D_MODEL, 0),
                   pl.BlockSpec((None, SUBLANES, B_WIDTH), lambda i, j: (i, 0, 0))),
        scratch_shapes=[pltpu.VMEM((SUBLANES, B_WIDTH), F32)],
        compiler_params=_params(("arbitrary", "arbitrary")),
        name="even_tail",
    )(proj, proj, proj, proj, ya, ym, conv_w, conv_b, halo, x, w_out, ln_g, ln_b)


def _odd_tail_body(gate_ref, oc_ref, hc_ref, yd_ref, ym_ref, nw_ref, x_ref, w_ref, g_ref, b_ref, o_ref, o16_ref):
    for rows in _sub_blocks(x_ref.shape[0]):
        sg = _silu(gate_ref[rows, :])
        yc = hc_ref[rows, :] * jax.nn.sigmoid(oc_ref[rows, :]) * sg[:, 0:C_WIDTH]
        z = yd_ref[rows, :] * sg[:, C_WIDTH:C_WIDTH + D_WIDTH]
        yd = z * lax.rsqrt(jnp.mean(z * z, axis=-1, keepdims=True) + EPS) * nw_ref[...]
        parts = (yc, yd, ym_ref[rows, :] * sg[:, C_WIDTH + D_WIDTH:MIX_WIDTH])
        _residual_ln(_mix_matmul(parts, w_ref), rows, x_ref, g_ref, b_ref, o_ref, o16_ref)


def odd_tail(proj, hc, yd, ym, x, norm_w, norm_layer, w_out, ln_g, ln_b, layer, *, tm):
    m = x.shape[0]
    rows = lambda width, col: pl.BlockSpec((tm, width), lambda i, j: (i, col))
    return pl.pallas_call(
        _odd_tail_body,
        out_shape=(jax.ShapeDtypeStruct((m, D_MODEL), F32), jax.ShapeDtypeStruct((m, D_MODEL), BF16)),
        grid=(m // tm, 1),
        in_specs=[
            rows(MIX_WIDTH, 0), rows(C_WIDTH, ODD_OC), rows(C_WIDTH, 0), rows(D_WIDTH, 0), rows(M_WIDTH, 0),
            _layer_param((1, D_WIDTH), norm_layer),
            rows(D_MODEL, 0), _out_weight(layer), _layer_param((1, D_MODEL), layer), _layer_param((1, D_MODEL), layer),
        ],
        out_specs=(rows(D_MODEL, 0), rows(D_MODEL, 0)),
        compiler_params=_params(("parallel", "arbitrary")),
        name="odd_tail",
    )(proj, proj, hc, yd, ym, norm_w, x, w_out, ln_g, ln_b)


def _tri_masks(t):
    ri = lax.broadcasted_iota(jnp.int32, (t, t), 0)
    ci = lax.broadcasted_iota(jnp.int32, (t, t), 1)
    return ri >= ci, ri <= ci


def _cumsum_cols(lower, x):
    return jnp.dot(lower.astype(F32), x, precision=lax.Precision.HIGHEST, preferred_element_type=F32)


def _cumsum_rows(upper, x):
    return jnp.dot(x, upper.astype(F32), precision=lax.Precision.HIGHEST, preferred_element_type=F32)


def _prefix_max_rows(x):
    row = lax.broadcasted_iota(jnp.int32, x.shape, 0)
    s = 1
    while s < x.shape[0]:
        x = jnp.where(row >= s, jnp.maximum(x, pltpu.roll(x, s, 0)), x)
        s *= 2
    return x


def _state_spec(shape, layer):
    return pl.BlockSpec((None, 1) + shape, lambda i, j: (layer, i) + (0,) * len(shape))


def _state_out_spec(shape):
    return pl.BlockSpec((1,) + shape, lambda i, j: (i,) + (0,) * len(shape))


def _mlstm_body(q_ref, k_ref, v_ref, gc_ref, gr_ref, c0_ref, n0_ref, m0_ref,
                h_ref, c_ref, n_ref, m_ref):
    @pl.when(pl.program_id(1) == 0)
    def _():
        c_ref[...] = c0_ref[...]
        n_ref[...] = n0_ref[...]
        m_ref[...] = m0_ref[...]

    t = q_ref.shape[0]
    scale = C_HEAD_DIM ** -0.5
    lower, upper = _tri_masks(t)
    gc = gc_ref[...]
    gr = gr_ref[...]
    b_c = pltpu.roll(_cumsum_cols(lower, _log_sigmoid(gc)), GATE_COLS - FG_OFF, 1)
    b_r = _cumsum_rows(upper, _log_sigmoid(gr))[FG_OFF:FG_OFF + C_HEADS, :]
    c_c = gc - b_c
    c_r = gr[IG_OFF:IG_OFF + C_HEADS, :] - b_r
    m8 = m_ref[0]
    eye = lax.broadcasted_iota(jnp.int32, m8.shape, 0) == lax.broadcasted_iota(jnp.int32, m8.shape, 1)
    ms_row = jnp.sum(jnp.where(eye, m8, 0.0), axis=0, keepdims=True)
    big_m = jnp.maximum(ms_row, _prefix_max_rows(c_c))
    gw_all = jnp.exp(ms_row - big_m)
    floor_all = jnp.exp(-(b_c + big_m))
    m_last = big_m[t - 1:t, :]
    wj_all = jnp.exp(c_c - m_last)
    decay_all = jnp.exp(ms_row - m_last)
    m_new_all = b_c[t - 1:t, :] + m_last
    for h in range(C_HEADS):
        hs = slice(h * C_HEAD_DIM, (h + 1) * C_HEAD_DIM)
        col = lambda a: a[:, h:h + 1]
        qf = q_ref[:, hs]
        kf = k_ref[:, hs] * scale
        vb = v_ref[:, hs].astype(BF16)
        w = _fine_nt_dot(qf, kf) * jnp.exp(jnp.where(lower, c_r[h:h + 1, :] - col(big_m), -jnp.inf))
        gw = col(gw_all)
        cs = c_ref[0, h]
        ns = n_ref[0, h:h + 1, :]
        w_hi, w_lo = _split_bf16(w)
        num = _dot(w_hi, vb) + _dot(w_lo, vb) + gw * _fine_dot(qf, cs)
        nq = jnp.sum(w, axis=1, keepdims=True) + gw * jnp.sum(qf * ns, axis=1, keepdims=True)
        h_ref[:, hs] = num * (1.0 / jnp.maximum(jnp.abs(nq), col(floor_all)))
        decay = col(decay_all)
        kw = kf * col(wj_all)
        kw_hi, kw_lo = _split_bf16(kw)
        c_ref[0, h] = decay * cs + _tn_dot(kw_hi, vb) + _tn_dot(kw_lo, vb)
        n_ref[0, h:h + 1, :] = decay * ns + jnp.sum(kw, axis=0, keepdims=True)
        m_ref[0, h:h + 1, :] = jnp.broadcast_to(col(m_new_all), (1, C_HEAD_DIM))


def mlstm(proj, gates_c, gates_r, c0, n0, m0, layer, *, batch, seq, t):
    nb = seq // t
    row = lambda i, j: i * nb + j
    rows = lambda width, col: pl.BlockSpec((t, width), lambda i, j: (row(i, j), col))
    shapes = (c0.shape[2:], n0.shape[2:], m0.shape[2:])
    return pl.pallas_call(
        _mlstm_body,
        out_shape=(jax.ShapeDtypeStruct((batch * seq, C_WIDTH), F32),)
        + tuple(jax.ShapeDtypeStruct((batch,) + s, F32) for s in shapes),
        grid=(batch, nb),
        in_specs=[
            rows(C_WIDTH, ODD_QC), rows(C_WIDTH, ODD_KC), rows(C_WIDTH, ODD_VC), rows(GATE_COLS, 0),
            pl.BlockSpec((None, N_GATES, t), lambda i, j: (row(i, j), 0, 0)),
        ] + [_state_spec(s, layer) for s in shapes],
        out_specs=(rows(C_WIDTH, 0),) + tuple(_state_out_spec(s) for s in shapes),
        compiler_params=_params(("arbitrary", "arbitrary")),
        name="mlstm",
    )(proj, proj, proj, gates_c, gates_r, c0, n0, m0)


def _ssd_body(xd_ref, bc_ref, gc_ref, gr_ref, wx_ref, wbc_ref, bx_ref, bbc_ref,
              alr_ref, alc_ref, dskip_ref, hx0_ref, hbc0_ref, s0_ref,
              y_ref, s_ref, hxo_ref, hbco_ref, hx, hbc):
    @pl.when(pl.program_id(1) == 0)
    def _():
        s_ref[...] = s0_ref[...]
        hx[...] = hx0_ref[0]
        hbc[...] = hbc0_ref[0]

    t = xd_ref.shape[0]
    xr = xd_ref[...]
    br = bc_ref[...]
    xs = _silu(_causal_dwconv(xr, hx[...], wx_ref, bx_ref))
    bcs = _silu(_causal_dwconv(br, hbc[...], wbc_ref, bbc_ref))
    hx[...] = xr[t - SUBLANES:t]
    hbc[...] = br[t - SUBLANES:t]
    hxo_ref[0] = xr[t - SUBLANES:t]
    hbco_ref[0] = br[t - SUBLANES:t]

    lower, upper = _tri_masks(t)
    dt_c = _softplus(gc_ref[:, DT_OFF:DT_OFF + D_HEADS])
    dt_r = _softplus(gr_ref[DT_OFF:DT_OFF + D_HEADS, :])
    a_c = _cumsum_cols(lower, dt_c * -jnp.exp(alr_ref[...]))
    a_r = _cumsum_rows(upper, dt_r * -jnp.exp(alc_ref[...]))
    for g in range(D_GROUPS):
        bmg = bcs[:, g * D_STATE:(g + 1) * D_STATE].astype(BF16)
        cmg = bcs[:, (D_GROUPS + g) * D_STATE:(D_GROUPS + g + 1) * D_STATE].astype(BF16)
        cb = _nt_dot(cmg, bmg)
        for hh in range(D_HG):
            h = g * D_HG + hh
            hs = slice(h * D_HEAD_DIM, (h + 1) * D_HEAD_DIM)
            acb = jnp.broadcast_to(a_c[:, h:h + 1], (t, max(t, D_HEAD_DIM)))
            ace = acb[:, :D_HEAD_DIM]
            lw = jnp.exp(jnp.where(lower, acb[:, :t] - a_r[h:h + 1, :], -jnp.inf))
            xh = xs[:, hs]
            sh = s_ref[0, h]
            y = _dot((cb * (lw * dt_r[h:h + 1, :])).astype(BF16), xh.astype(BF16))
            y = y + jnp.exp(ace) * _nt_dot(cmg, sh.astype(BF16))
            y_ref[:, hs] = y + dskip_ref[:, hs] * xh
            at = a_c[t - 1:t, h:h + 1]
            wj = jnp.exp(at - ace) * dt_c[:, h:h + 1]
            s_ref[0, h] = jnp.exp(at) * sh + _tn_dot((xh * wj).astype(BF16), bmg)


def ssd(proj, gates_c, gates_r, wx, wbc, bx, bbc, alog_r, alog_c, dskip, hx0, hbc0, s0, layer, state_layer,
        *, batch, seq, t):
    nb = seq // t
    row = lambda i, j: i * nb + j
    rows = lambda width, col: pl.BlockSpec((t, width), lambda i, j: (row(i, j), col))
    param = lambda a: _layer_param(a.shape[1:], layer)
    shapes = (s0.shape[2:], hx0.shape[2:], hbc0.shape[2:])
    return pl.pallas_call(
        _ssd_body,
        out_shape=(jax.ShapeDtypeStruct((batch * seq, D_WIDTH), F32),)
        + tuple(jax.ShapeDtypeStruct((batch,) + s, F32) for s in shapes),
        grid=(batch, nb),
        in_specs=[
            rows(D_WIDTH, ODD_XD), rows(D_BC, ODD_BC_BLOCK),
            rows(GATE_COLS, 0), pl.BlockSpec((None, N_GATES, t), lambda i, j: (row(i, j), 0, 0)),
            param(wx), param(wbc), param(bx), param(bbc), param(alog_r), param(alog_c), param(dskip),
            _state_spec(shapes[1], state_layer), _state_spec(shapes[2], state_layer), _state_spec(shapes[0], state_layer),
        ],
        out_specs=(rows(D_WIDTH, 0),) + tuple(_state_out_spec(s) for s in shapes),
        scratch_shapes=[pltpu.VMEM((SUBLANES, D_WIDTH), F32), pltpu.VMEM((SUBLANES, D_BC), F32)],
        compiler_params=_params(("arbitrary", "arbitrary")),
        name="ssd",
    )(proj, proj, gates_c, gates_r, wx, wbc, bx, bbc, alog_r, alog_c, dskip, hx0, hbc0, s0)


def _pad_state_rows(buf):
    pad = jnp.zeros(buf.shape[:-2] + (SUBLANES - buf.shape[-2], buf.shape[-1]), F32)
    return jnp.concatenate([pad, buf], axis=-2)


def _row_gates(gates_c, t):
    m = gates_c.shape[0]
    return gates_c[:, :N_GATES].reshape(m // t, t, N_GATES).transpose(0, 2, 1)


def _even_weights(w_in, b_in):
    off = A_WIDTH * 3 + B_WIDTH * 3
    order = lambda a: jnp.concatenate([a[..., off + M_WIDTH:], a[..., off:off + M_WIDTH], a[..., :off]], axis=-1)
    return order(w_in).astype(BF16), order(b_in)[:, None, :]


def _odd_weights(w_in, b_in):
    c4 = 4 * C_WIDTH
    g0 = c4 + 2 * C_HEADS
    d0 = g0 + D_XBC
    q0 = d0 + D_HEADS
    t0 = q0 + M_WIDTH
    big = lambda a: jnp.concatenate([a[..., t0:], a[..., q0:t0], a[..., :c4], a[..., g0:d0]], axis=-1)
    small = lambda a: jnp.concatenate(
        [a[..., c4:g0], a[..., d0:q0], jnp.zeros(a.shape[:-1] + (GATE_COLS - N_GATES,), a.dtype)], axis=-1)
    qk_lo = _split_bf16(w_in[..., :2 * C_WIDTH])[1]
    small_hi, small_lo = _split_bf16(small(w_in))
    return (big(w_in).astype(BF16), big(b_in)[:, None, :], qk_lo), (small_hi, small(b_in)[:, None, :], small_lo)


def _heads(a, batch, seq, col, last):
    part = a.reshape(batch, seq, a.shape[1])[:, seq - last:, col * A_WIDTH:(col + 1) * A_WIDTH]
    return part.reshape(batch, last, A_HEADS, A_HEAD_DIM)


def kernel(x_prompt, x_sample, mem_prompt, cache_a_k, cache_a_v, cache_mem_k, cache_mem_v, state_b_conv, state_c_C, state_c_n, state_c_m, state_d_ssm, state_d_conv, w_in_even, b_in_even, w_in_odd, b_in_odd, w_out, w_mem_kv, ln_g, ln_b, rel_bias_table, b_conv_w, b_conv_b, d_conv_w, d_conv_b, d_A_log, d_skip, d_norm_w):
    bp, sp, _ = x_prompt.shape
    bs, ss, _ = x_sample.shape
    n_even, n_odd = w_in_even.shape[0], w_in_odd.shape[0]
    lc = cache_a_k.shape[2]
    xp = x_prompt.reshape(bp * sp, D_MODEL)
    xs = x_sample.reshape(bs * ss, D_MODEL)
    xp16, xs16 = xp.astype(BF16), xs.astype(BF16)
    mem = mem_prompt.reshape(bp * MEM_LEN, D_MODEL).astype(BF16)

    we, be = _even_weights(w_in_even, b_in_even)
    (wo_big, bo_big, wo_qk_lo), (wo_small, bo_small, wo_small_lo) = _odd_weights(w_in_odd, b_in_odd)
    qk_blocks = (ODD_QC * C_WIDTH // 512, (ODD_KC + 1) * C_WIDTH // 512)
    w_out_b = w_out.astype(BF16)
    w_mem_b = w_mem_kv.astype(BF16)
    zero_bias = jnp.zeros((1, 1, 2 * M_WIDTH), F32)
    ln_g3, ln_b3 = ln_g[:, None, :], ln_b[:, None, :]
    conv_b3 = b_conv_b[:, None, :]
    norm_w3 = d_norm_w[:, None, :]
    d_wx, d_wbc = d_conv_w[:, :, :D_WIDTH], d_conv_w[:, :, D_WIDTH:]
    d_bx, d_bbc = d_conv_b[:, None, :D_WIDTH], d_conv_b[:, None, D_WIDTH:]
    alog_r, alog_c = d_A_log[:, None, :], d_A_log[:, :, None]
    dskip = jnp.repeat(d_skip, D_HEAD_DIM, axis=-1)[:, None, :]

    s_halo_b = _pad_state_rows(state_b_conv)
    s_halo_d = _pad_state_rows(state_d_conv)
    s_hx0, s_hbc0 = s_halo_d[..., :D_WIDTH], s_halo_d[..., D_WIDTH:]
    s_m0 = jnp.broadcast_to(state_c_m[..., None], state_c_m.shape + (C_HEAD_DIM,))
    zeros = lambda *shape: jnp.zeros((1, bp) + shape, F32)
    p_halo_b, p_hx0, p_hbc0 = zeros(SUBLANES, B_WIDTH), zeros(SUBLANES, D_WIDTH), zeros(SUBLANES, D_BC)
    p_c0, p_n0, p_m0 = zeros(C_HEADS, C_HEAD_DIM, C_HEAD_DIM), zeros(C_HEADS, C_HEAD_DIM), zeros(C_HEADS, C_HEAD_DIM)
    p_s0 = zeros(D_HEADS, D_HEAD_DIM, D_STATE)

    outs = {k: [] for k in ("p_mk", "p_mv", "p_ak", "p_av", "p_bc", "p_cc", "p_cn", "p_cm", "p_ds", "p_dc",
                            "s_ak", "s_av", "s_bc", "s_cc", "s_cn", "s_cm", "s_ds", "s_dc")}
    tp = min(SCAN_BLOCK, sp)
    ts = min(SCAN_BLOCK, ss)
    for l in range(DEPTH):
        i = l // 2
        mkv = linear(mem, w_mem_b, zero_bias.repeat(DEPTH, 0), l)
        mkv5 = mkv.reshape(1, bp, MEM_LEN, 2 * M_HEADS, M_HEAD_DIM)
        mk5, mv5 = mkv5[:, :, :, :M_HEADS], mkv5[:, :, :, M_HEADS:]
        outs["p_mk"].append(mk5[0])
        outs["p_mv"].append(mv5[0])
        flat = lambda a: a.reshape(a.shape[0], a.shape[1], MEM_LEN * M_HEADS, M_HEAD_DIM)
        p_mem = (flat(mk5), flat(mv5), lambda b: (0, b, 0, 0), lambda b: (0, b, 0, 0))
        s_mem = (flat(cache_mem_k), flat(cache_mem_v), lambda b, l=l: (l, b, 0, 0), lambda b, l=l: (l, b, 0, 0))
        if l % 2 == 0:
            proj_p = linear(xp16, we, be, i)
            proj_s = linear(xs16, we, be, i)
            ya_p = band_attention_prompt(proj_p, rel_bias_band(rel_bias_table[i]), tq=A_WINDOW)
            bias_s = rel_bias(rel_bias_table[i], ss, lc + ss, lc)
            rows_of = lambda a: a.reshape(a.shape[0], a.shape[1], lc * A_HEADS, A_HEAD_DIM)
            ya_s = band_attention_sample(proj_s, rows_of(cache_a_k), rows_of(cache_a_v), i,
                                         bias_s[:, :, :lc], bias_s[:, :, lc:], t=ss)
            ym_p = mem_attention(proj_p, *p_mem, batch=bp, seq=sp, tl=min(sp, 512))
            ym_s = mem_attention(proj_s, *s_mem, batch=bs, seq=ss, tl=min(ss, 512))
            tail = (b_conv_w, conv_b3)
            xp_new, xp16_new, xplo_new, tail_p = even_tail(proj_p, ya_p, ym_p, xp, *tail, p_halo_b, 0, w_out_b, ln_g3, ln_b3,
                                                 i, l, seq=sp, tm=min(sp, 256))
            xs_new, xs16_new, xslo_new, tail_s = even_tail(proj_s, ya_s, ym_s, xs, *tail, s_halo_b, i, w_out_b, ln_g3, ln_b3,
                                                 i, l, seq=ss, tm=min(ss, 256))
            outs["p_ak"].append(_heads(proj_p, bp, sp, EVEN_KA, A_WINDOW))
            outs["p_av"].append(_heads(proj_p, bp, sp, EVEN_VA, A_WINDOW))
            outs["s_ak"].append(_heads(proj_s, bs, ss, EVEN_KA, ss))
            outs["s_av"].append(_heads(proj_s, bs, ss, EVEN_VA, ss))
            outs["p_bc"].append(tail_p[:, SUBLANES - (B_CONV - 1):])
            outs["s_bc"].append(tail_s[:, SUBLANES - (B_CONV - 1):])
        else:
            ssd_params = (d_wx, d_wbc, d_bx, d_bbc, alog_r, alog_c, dskip)
            res = []
            for x, x16, xlo, batch, seq, t, mem_args, states, sl in (
                    (xp, xp16, xplo, bp, sp, tp, p_mem, (p_c0, p_n0, p_m0, p_hx0, p_hbc0, p_s0), 0),
                    (xs, xs16, xslo, bs, ss, ts, s_mem, (state_c_C, state_c_n, s_m0, s_hx0, s_hbc0, state_d_ssm), i)):
                c0, n0, m0, hx0, hbc0, s0 = states
                proj = linear(x16, wo_big, bo_big, i, fine=(xlo, wo_qk_lo, *qk_blocks) if i == 0 else None)
                gates_c = linear(x16, wo_small, bo_small, i, tn=GATE_COLS, fine=(xlo, wo_small_lo, 0, 1))
                gates_r = _row_gates(gates_c, t)
                hc, cc, cn, cm = mlstm(proj, gates_c, gates_r, c0, n0, m0, sl, batch=batch, seq=seq, t=t)
                yd, s_new, hx, hbc = ssd(proj, gates_c, gates_r, *ssd_params, hx0, hbc0, s0, i, sl,
                                         batch=batch, seq=seq, t=t)
                ym = mem_attention(proj, *mem_args, batch=batch, seq=seq, tl=min(seq, 512))
                x_new, x16_new = odd_tail(proj, hc, yd, ym, x, norm_w3, i, w_out_b, ln_g3, ln_b3, l, tm=256)
                dbuf = jnp.concatenate([hx, hbc], axis=-1)[:, SUBLANES - (D_CONV - 1):]
                res.append((x_new, x16_new, cc, cn, cm[:, :, 0], s_new, dbuf))
            (xp_new, xp16_new, *st_p), (xs_new, xs16_new, *st_s) = res
            for key, val in zip(("p_cc", "p_cn", "p_cm", "p_ds", "p_dc"), st_p):
                outs[key].append(val)
            for key, val in zip(("s_cc", "s_cn", "s_cm", "s_ds", "s_dc"), st_s):
                outs[key].append(val)
        xp, xs, xp16, xs16 = xp_new, xs_new, xp16_new, xs16_new
        if l % 2 == 0:
            xplo, xslo = xplo_new, xslo_new
    st = jnp.stack
    return (xp.reshape(bp, sp, D_MODEL), xs.reshape(bs, ss, D_MODEL),
            st(outs["p_mk"]), st(outs["p_mv"]), st(outs["p_ak"]), st(outs["p_av"]), st(outs["p_bc"]),
            st(outs["p_cc"]), st(outs["p_cn"]), st(outs["p_cm"]), st(outs["p_ds"]), st(outs["p_dc"]),
            st(outs["s_ak"]), st(outs["s_av"]), st(outs["s_bc"]),
            st(outs["s_cc"]), st(outs["s_cn"]), st(outs["s_cm"]), st(outs["s_ds"]), st(outs["s_dc"]))
```

```python
import functools

import jax
import jax.numpy as jnp
from jax import lax
from jax.experimental import pallas as pl
from jax.experimental.pallas import tpu as pltpu

F32 = jnp.float32
BF16 = jnp.bfloat16

D_MODEL = 2048
DEPTH = 4
CHUNK = 64
ALPHA = (2 * DEPTH) ** 0.25
EPS = 1e-5
NEG = -1e30

A_WIDTH = D_MODEL // 2
A_HEAD_DIM = 128
A_HEADS = A_WIDTH // A_HEAD_DIM
A_BAND = 8
A_WINDOW = A_BAND * CHUNK
REL_CLIP = 128
B_WIDTH = D_MODEL // 2
B_CONV = 3
C_WIDTH = D_MODEL // 2
C_HEAD_DIM = 128
C_HEADS = C_WIDTH // C_HEAD_DIM
D_WIDTH = D_MODEL // 2
D_HEAD_DIM = 64
D_HEADS = D_WIDTH // D_HEAD_DIM
D_GROUPS = 2
D_HG = D_HEADS // D_GROUPS
D_STATE = 128
D_CONV = 4
D_BC = 2 * D_GROUPS * D_STATE
D_XBC = D_WIDTH + D_BC
MEM_LEN = 256
M_WIDTH = D_MODEL // 4
M_HEADS = 4
M_HEAD_DIM = M_WIDTH // M_HEADS
MIX_WIDTH = A_WIDTH + B_WIDTH + M_WIDTH

LANES = 128
SUBLANES = 8
VMEM_LIMIT_BYTES = 56 * 1024 * 1024

EVEN_QA, EVEN_KA, EVEN_VA, EVEN_BX, EVEN_BGATE, EVEN_CGATE = 3, 4, 5, 6, 7, 8
ODD_QC, ODD_KC, ODD_VC, ODD_OC, ODD_XD = 3, 4, 5, 6, 7
QM_BLOCK = MIX_WIDTH // M_WIDTH
ODD_BC_BLOCK = (MIX_WIDTH + M_WIDTH + 5 * 1024) // D_BC
GATE_COLS = LANES
IG_OFF, FG_OFF, DT_OFF = 0, C_HEADS, 2 * C_HEADS
N_GATES = 2 * C_HEADS + D_HEADS

SCAN_BLOCK = 256
Q_SUB = 256
K_SUB = Q_SUB + A_WINDOW
BAND_ROWS = (A_BAND + 1) * CHUNK
LOG2E = 1.4426950408889634
LINEAR_ROWS = 2048
TAIL_SUB = 128
MEM_GROUP = 4


def _params(semantics):
    return pltpu.CompilerParams(dimension_semantics=semantics, vmem_limit_bytes=VMEM_LIMIT_BYTES)


def _nt_dot(a, b):
    return lax.dot_general(a, b, (((1,), (1,)), ((), ())), preferred_element_type=F32)


def _tn_dot(a, b):
    return lax.dot_general(a, b, (((0,), (0,)), ((), ())), preferred_element_type=F32)


def _dot(a, b):
    return jnp.dot(a, b, preferred_element_type=F32)


def _split_dot(dot, a, b):
    return dot(a[0], b[0]) + (dot(a[0], b[1]) + dot(a[1], b[0]))


def _silu(x):
    return x * jax.nn.sigmoid(x)


def _softplus(x):
    return jnp.maximum(x, 0.0) + jnp.log1p(jnp.exp(-jnp.abs(x)))


def _log_sigmoid(x):
    return -_softplus(-x)


def _softmax_pv(s, v, exp=jnp.exp):
    m = jnp.max(s, axis=-1, keepdims=True)
    p = exp(s - m)
    inv = 1.0 / jnp.sum(p, axis=-1, keepdims=True)
    return _dot(p.astype(BF16), v) * inv


def _split_bf16(a):
    hi = a.astype(BF16)
    return hi, (a - hi.astype(F32)).astype(BF16)


def _linear_body(x_ref, w_ref, b_ref, o_ref):
    o_ref[...] = _dot(x_ref[...], w_ref[...]) + b_ref[...]


def _linear_fine_body(x_ref, w_ref, b_ref, xl_ref, wl_ref, o_ref, *, fine):
    j = pl.program_id(1)
    o_ref[...] = _dot(x_ref[...], w_ref[...]) + b_ref[...]

    @pl.when((j >= fine[0]) & (j < fine[1]))
    def _():
        o_ref[...] += _dot(xl_ref[...], w_ref[...]) + _dot(x_ref[...], wl_ref[...])


def linear(x, w, b, layer, *, tn=512, fine=None, col_map=lambda j: j):
    m, k = x.shape
    n = w.shape[2]
    tm = min(m, LINEAR_ROWS)
    assert m % tm == 0 and n % tn == 0, (m, n, tm, tn)
    in_specs = [
        pl.BlockSpec((tm, k), lambda i, j: (i, 0)),
        pl.BlockSpec((None, k, tn), lambda i, j: (layer, 0, col_map(j))),
        pl.BlockSpec((None, 1, tn), lambda i, j: (layer, 0, col_map(j))),
    ]
    body, args = _linear_body, (x, w, b)
    if fine is not None:
        x_lo, w_lo, j0, j1 = fine
        assert w_lo.shape[2] == (j1 - j0) * tn
        in_specs += [
            pl.BlockSpec((tm, k), lambda i, j: (i, 0)),
            pl.BlockSpec((None, k, tn), lambda i, j: (layer, 0, jnp.clip(j - j0, 0, j1 - j0 - 1))),
        ]
        body, args = functools.partial(_linear_fine_body, fine=(j0, j1)), (x, w, b, x_lo, w_lo)
    return pl.pallas_call(
        body,
        out_shape=jax.ShapeDtypeStruct((m, n), F32),
        grid=(m // tm, n // tn),
        in_specs=in_specs,
        out_specs=pl.BlockSpec((tm, tn), lambda i, j: (i, j)),
        compiler_params=_params(("parallel", "arbitrary")),
        name="linear",
    )(*args)


def _table_select(tab_ref, h, idx):
    def step(t, acc):
        return jnp.where(idx == t, tab_ref[h, t], acc)

    return lax.fori_loop(0, 2 * REL_CLIP + 1, step, jnp.zeros(idx.shape, F32))


def _relbias_body(tab_ref, o_ref, *, offset):
    ql, kl = o_ref.shape[1], o_ref.shape[2]
    qi = lax.broadcasted_iota(jnp.int32, (ql, kl), 0)
    ki = lax.broadcasted_iota(jnp.int32, (ql, kl), 1)
    idx = jnp.clip(qi + offset - ki, -REL_CLIP, REL_CLIP) + REL_CLIP
    o_ref[0] = _table_select(tab_ref, pl.program_id(0), idx)


def rel_bias(table, q_len, k_len, offset):
    heads = table.shape[0]
    return pl.pallas_call(
        functools.partial(_relbias_body, offset=offset),
        out_shape=jax.ShapeDtypeStruct((heads, q_len, k_len), F32),
        grid=(heads,),
        in_specs=[pl.BlockSpec(memory_space=pltpu.SMEM)],
        out_specs=pl.BlockSpec((1, q_len, k_len), lambda h: (h, 0, 0)),
        compiler_params=_params(("arbitrary",)),
        name="rel_bias",
    )(table)


def _relbias_band_body(tab_ref, o_ref):
    period = 2 * A_WINDOW
    u = lax.broadcasted_iota(jnp.int32, (SUBLANES, period), 1)
    d = jnp.where(u < K_SUB, u, u - period)
    idx = jnp.clip(A_WINDOW - d, -REL_CLIP, REL_CLIP) + REL_CLIP
    g = _table_select(tab_ref, pl.program_id(0), idx)
    full = pltpu.roll(jnp.broadcast_to(g[0:1], (Q_SUB, period)), 0, 1, stride=1, stride_axis=0)
    r = lax.broadcasted_iota(jnp.int32, (Q_SUB, K_SUB), 0)
    w = lax.broadcasted_iota(jnp.int32, (Q_SUB, K_SUB), 1)
    first = (r // CHUNK) * CHUNK
    in_band = (w >= first) & (w < first + BAND_ROWS)
    o_ref[0] = jnp.where(in_band, full[:, :K_SUB] * LOG2E, NEG)


def rel_bias_band(table):
    heads = table.shape[0]
    return pl.pallas_call(
        _relbias_band_body,
        out_shape=jax.ShapeDtypeStruct((heads, Q_SUB, K_SUB), F32),
        grid=(heads,),
        in_specs=[pl.BlockSpec(memory_space=pltpu.SMEM)],
        out_specs=pl.BlockSpec((1, Q_SUB, K_SUB), lambda h: (h, 0, 0)),
        compiler_params=_params(("arbitrary",)),
        name="rel_bias_band",
    )(table)


def _band_prompt_body(q_ref, k_ref, v_ref, bias_ref, o_ref, kwin, vwin):
    i = pl.program_id(0)
    tq = q_ref.shape[0]
    scale = A_HEAD_DIM ** -0.5 * LOG2E

    @pl.when(i == 0)
    def _():
        kwin[0:tq] = jnp.zeros((tq, A_WIDTH), BF16)
        vwin[0:tq] = jnp.zeros((tq, A_WIDTH), BF16)

    @pl.when(i > 0)
    def _():
        kwin[0:tq] = kwin[tq:2 * tq]
        vwin[0:tq] = vwin[tq:2 * tq]

    kwin[tq:2 * tq] = k_ref[...].astype(BF16)
    vwin[tq:2 * tq] = v_ref[...].astype(BF16)

    col = lax.broadcasted_iota(jnp.int32, (1, K_SUB), 1)
    unreal = jnp.where(i == 0, tq, 0)
    for r0 in range(0, tq, Q_SUB):
        neg = jnp.where(col + r0 < unreal, NEG, 0.0)
        for h in range(A_HEADS):
            hs = slice(h * A_HEAD_DIM, (h + 1) * A_HEAD_DIM)
            qh = (q_ref[r0:r0 + Q_SUB, hs] * scale).astype(BF16)
            s = _nt_dot(qh, kwin[r0:r0 + K_SUB, hs]) + bias_ref[h] + neg
            o_ref[r0:r0 + Q_SUB, hs] = _softmax_pv(s, vwin[r0:r0 + K_SUB, hs], jnp.exp2)


def band_attention_prompt(proj, bias, *, tq):
    s = proj.shape[0]
    assert tq >= A_WINDOW and tq % Q_SUB == 0 and Q_SUB % CHUNK == 0 and s % tq == 0
    return pl.pallas_call(
        _band_prompt_body,
        out_shape=jax.ShapeDtypeStruct((s, A_WIDTH), F32),
        grid=(s // tq,),
        in_specs=[
            pl.BlockSpec((tq, A_WIDTH), lambda i: (i, EVEN_QA)),
            pl.BlockSpec((tq, A_WIDTH), lambda i: (i, EVEN_KA)),
            pl.BlockSpec((tq, A_WIDTH), lambda i: (i, EVEN_VA)),
            pl.BlockSpec((A_HEADS, Q_SUB, K_SUB), lambda i: (0, 0, 0)),
        ],
        out_specs=pl.BlockSpec((tq, A_WIDTH), lambda i: (i, 0)),
        scratch_shapes=[pltpu.VMEM((2 * tq, A_WIDTH), BF16), pltpu.VMEM((2 * tq, A_WIDTH), BF16)],
        compiler_params=_params(("arbitrary",)),
        name="band_attention_prompt",
    )(proj, proj, proj, bias)


def _head_rows(ref, h, heads):
    return ref[pl.ds(h, ref.shape[0] // heads, stride=heads), :]


def _band_sample_body(q_ref, nk_ref, nv_ref, ck_ref, cv_ref, bc_ref, bn_ref, o_ref):
    scale = A_HEAD_DIM ** -0.5
    for h in range(A_HEADS):
        hs = slice(h * A_HEAD_DIM, (h + 1) * A_HEAD_DIM)
        qh = q_ref[:, hs].astype(BF16)
        s1 = _nt_dot(qh, _head_rows(ck_ref, h, A_HEADS).astype(BF16)) * scale + bc_ref[h]
        s2 = _nt_dot(qh, nk_ref[:, hs].astype(BF16)) * scale + bn_ref[h]
        m = jnp.maximum(jnp.max(s1, axis=-1, keepdims=True), jnp.max(s2, axis=-1, keepdims=True))
        p1 = jnp.exp(s1 - m)
        p2 = jnp.exp(s2 - m)
        inv = 1.0 / (jnp.sum(p1, axis=-1, keepdims=True) + jnp.sum(p2, axis=-1, keepdims=True))
        o = _dot(p1.astype(BF16), _head_rows(cv_ref, h, A_HEADS).astype(BF16))
        o = o + _dot(p2.astype(BF16), nv_ref[:, hs].astype(BF16))
        o_ref[:, hs] = o * inv


def band_attention_sample(proj, cache_k, cache_v, layer, bias_cache, bias_new, *, t):
    _, b, rows, _ = cache_k.shape
    lc = rows // A_HEADS
    cache = pl.BlockSpec((None, None, rows, A_HEAD_DIM), lambda i: (layer, i, 0, 0))
    return pl.pallas_call(
        _band_sample_body,
        out_shape=jax.ShapeDtypeStruct((b * t, A_WIDTH), F32),
        grid=(b,),
        in_specs=[
            pl.BlockSpec((t, A_WIDTH), lambda i: (i, EVEN_QA)),
            pl.BlockSpec((t, A_WIDTH), lambda i: (i, EVEN_KA)),
            pl.BlockSpec((t, A_WIDTH), lambda i: (i, EVEN_VA)),
            cache, cache,
            pl.BlockSpec((A_HEADS, t, lc), lambda i: (0, 0, 0)),
            pl.BlockSpec((A_HEADS, t, t), lambda i: (0, 0, 0)),
        ],
        out_specs=pl.BlockSpec((t, A_WIDTH), lambda i: (i, 0)),
        compiler_params=_params(("parallel",)),
        name="band_attention_sample",
    )(proj, proj, proj, cache_k, cache_v, bias_cache, bias_new)


def _mem_attn_body(q_ref, mk_ref, mv_ref, o_ref):
    scale = M_HEAD_DIM ** -0.5
    group = mk_ref.shape[0]
    tl = q_ref.shape[0] // group
    for g in range(group):
        rows = slice(g * tl, (g + 1) * tl)
        for h in range(M_HEADS):
            hs = slice(h * M_HEAD_DIM, (h + 1) * M_HEAD_DIM)
            s = _nt_dot(q_ref[rows, hs].astype(BF16), _head_rows(mk_ref.at[g], h, M_HEADS).astype(BF16)) * scale
            o_ref[rows, hs] = _softmax_pv(s, _head_rows(mv_ref.at[g], h, M_HEADS).astype(BF16))


def mem_attention(proj, mk, mv, kmap, vmap, *, batch, seq, tl):
    nt = seq // tl
    group = MEM_GROUP if (nt == 1 and batch % MEM_GROUP == 0) else 1
    mem = lambda f: pl.BlockSpec((None, group, MEM_LEN * M_HEADS, M_HEAD_DIM), lambda i, j: f(i))
    return pl.pallas_call(
        _mem_attn_body,
        out_shape=jax.ShapeDtypeStruct((batch * seq, M_WIDTH), F32),
        grid=(batch // group, nt),
        in_specs=[pl.BlockSpec((group * tl, M_WIDTH), lambda i, j: (i * nt + j, QM_BLOCK)), mem(kmap), mem(vmap)],
        out_specs=pl.BlockSpec((group * tl, M_WIDTH), lambda i, j: (i * nt + j, 0)),
        compiler_params=_params(("parallel", "parallel")),
        name="mem_attention",
    )(proj, mk, mv)


def _shift_rows(x, halo, s):
    rolled = pltpu.roll(x, s, 0)
    halo_rolled = pltpu.roll(halo, s, 0)
    row = lax.broadcasted_iota(jnp.int32, halo.shape, 0)
    top = jnp.where(row < s, halo_rolled, rolled[0:SUBLANES])
    if x.shape[0] == SUBLANES:
        return top
    return jnp.concatenate([top, rolled[SUBLANES:]], axis=0)


def _causal_dwconv(x, halo, w_ref, b_ref):
    width = w_ref.shape[0]
    y = b_ref[...]
    for j in range(width):
        s = width - 1 - j
        xs = x if s == 0 else _shift_rows(x, halo, s)
        y = y + xs * w_ref[j:j + 1, :]
    return y


def _residual_ln(acc, rows, x_ref, g_ref, b_ref, o_ref, o16_ref, olo_ref=None):
    y = ALPHA * x_ref[rows, :] + acc
    mu = jnp.mean(y, axis=-1, keepdims=True)
    d = y - mu
    var = jnp.mean(d * d, axis=-1, keepdims=True)
    out = d * lax.rsqrt(var + EPS) * g_ref[...] + b_ref[...]
    o_ref[rows, :] = out
    hi, lo = _split_bf16(out)
    o16_ref[rows, :] = hi
    if olo_ref is not None:
        olo_ref[rows, :] = lo


def _sub_blocks(tm):
    sub = min(TAIL_SUB, tm)
    return [slice(r, r + sub) for r in range(0, tm, sub)]


def _mix_matmul(parts, w_ref):
    acc, r0 = None, 0
    for p in parts:
        term = _dot(p.astype(BF16), w_ref[r0:r0 + p.shape[1], :])
        acc = term if acc is None else acc + term
        r0 += p.shape[1]
    return acc


def _even_tail_body(gate_ref, bx_ref, bg_ref, cg_ref, ya_ref, ym_ref, cw_ref, cb_ref, halo_ref,
                    x_ref, w_ref, g_ref, b_ref, o_ref, o16_ref, olo_ref, tail_ref, carry):
    @pl.when(pl.program_id(1) == 0)
    def _():
        carry[...] = halo_ref[...]

    tm = bx_ref.shape[0]
    conv_in = lambda rows: cg_ref[rows, :] * bx_ref[rows, :]
    for rows in _sub_blocks(tm):
        halo = carry[...] if rows.start == 0 else conv_in(slice(rows.start - SUBLANES, rows.start))
        yb = bg_ref[rows, :] * _causal_dwconv(conv_in(rows), halo, cw_ref, cb_ref)
        sg = _silu(gate_ref[rows, :])
        parts = (ya_ref[rows, :] * sg[:, 0:A_WIDTH],
                 yb * sg[:, A_WIDTH:A_WIDTH + B_WIDTH],
                 ym_ref[rows, :] * sg[:, A_WIDTH + B_WIDTH:MIX_WIDTH])
        _residual_ln(_mix_matmul(parts, w_ref), rows, x_ref, g_ref, b_ref, o_ref, o16_ref, olo_ref)
    last = conv_in(slice(tm - SUBLANES, tm))
    carry[...] = last
    tail_ref[...] = last


def _layer_param(shape, layer):
    return pl.BlockSpec((None,) + shape, lambda i, j: (layer,) + (0,) * len(shape))


def _out_weight(layer):
    return pl.BlockSpec((None, MIX_WIDTH, D_MODEL), lambda i, j: (layer, 0, 0), pipeline_mode=pl.Buffered(1))


def even_tail(proj, ya, ym, x, conv_w, conv_b, halo, halo_layer, w_out, ln_g, ln_b, conv_layer, layer, *, seq, tm):
    b = halo.shape[1]
    nt = seq // tm
    row = lambda i, j: i * nt + j
    rows = lambda width, col: pl.BlockSpec((tm, width), lambda i, j: (row(i, j), col))
    return pl.pallas_call(
        _even_tail_body,
        out_shape=(jax.ShapeDtypeStruct((b * seq, D_MODEL), F32),
                   jax.ShapeDtypeStruct((b * seq, D_MODEL), BF16),
                   jax.ShapeDtypeStruct((b * seq, D_MODEL), BF16),
                   jax.ShapeDtypeStruct((b, SUBLANES, B_WIDTH), F32)),
        grid=(b, nt),
        in_specs=[
            rows(MIX_WIDTH, 0), rows(B_WIDTH, EVEN_BX), rows(B_WIDTH, EVEN_BGATE), rows(B_WIDTH, EVEN_CGATE),
            rows(A_WIDTH, 0), rows(M_WIDTH, 0),
            _layer_param((B_CONV, B_WIDTH), conv_layer), _layer_param((1, B_WIDTH), conv_layer),
            pl.BlockSpec((None, None, SUBLANES, B_WIDTH), lambda i, j: (halo_layer, i, 0, 0)),
            rows(D_MODEL, 0), _out_weight(layer), _layer_param((1, D_MODEL), layer), _layer_param((1, D_MODEL), layer),
        ],
        out_specs=(rows(D_MODEL, 0), rows(D_MODEL, 0), rows(D_MODEL, 0),
                   pl.BlockSpec((None, SUBLANES, B_WIDTH), lambda i, j: (i, 0, 0))),
        scratch_shapes=[pltpu.VMEM((SUBLANES, B_WIDTH), F32)],
        compiler_params=_params(("arbitrary", "arbitrary")),
        name="even_tail",
    )(proj, proj, proj, proj, ya, ym, conv_w, conv_b, halo, x, w_out, ln_g, ln_b)


def _odd_tail_body(gate_ref, oc_ref, hc_ref, yd_ref, ym_ref, nw_ref, x_ref, w_ref, g_ref, b_ref, o_ref, o16_ref):
    for rows in _sub_blocks(x_ref.shape[0]):
        sg = _silu(gate_ref[rows, :])
        yc = hc_ref[rows, :] * jax.nn.sigmoid(oc_ref[rows, :]) * sg[:, 0:C_WIDTH]
        z = yd_ref[rows, :] * sg[:, C_WIDTH:C_WIDTH + D_WIDTH]
        yd = z * lax.rsqrt(jnp.mean(z * z, axis=-1, keepdims=True) + EPS) * nw_ref[...]
        parts = (yc, yd, ym_ref[rows, :] * sg[:, C_WIDTH + D_WIDTH:MIX_WIDTH])
        _residual_ln(_mix_matmul(parts, w_ref), rows, x_ref, g_ref, b_ref, o_ref, o16_ref)


def odd_tail(proj, hc, yd, ym, x, norm_w, norm_layer, w_out, ln_g, ln_b, layer, *, tm):
    m = x.shape[0]
    rows = lambda width, col: pl.BlockSpec((tm, width), lambda i, j: (i, col))
    return pl.pallas_call(
        _odd_tail_body,
        out_shape=(jax.ShapeDtypeStruct((m, D_MODEL), F32), jax.ShapeDtypeStruct((m, D_MODEL), BF16)),
        grid=(m // tm, 1),
        in_specs=[
            rows(MIX_WIDTH, 0), rows(C_WIDTH, ODD_OC), rows(C_WIDTH, 0), rows(D_WIDTH, 0), rows(M_WIDTH, 0),
            _layer_param((1, D_WIDTH), norm_layer),
            rows(D_MODEL, 0), _out_weight(layer), _layer_param((1, D_MODEL), layer), _layer_param((1, D_MODEL), layer),
        ],
        out_specs=(rows(D_MODEL, 0), rows(D_MODEL, 0)),
        compiler_params=_params(("parallel", "arbitrary")),
        name="odd_tail",
    )(proj, proj, hc, yd, ym, norm_w, x, w_out, ln_g, ln_b)


def _tri_masks(t):
    ri = lax.broadcasted_iota(jnp.int32, (t, t), 0)
    ci = lax.broadcasted_iota(jnp.int32, (t, t), 1)
    return ri >= ci, ri <= ci


def _cumsum_cols(lower, x):
    return jnp.dot(lower.astype(F32), x, precision=lax.Precision.HIGHEST, preferred_element_type=F32)


def _cumsum_rows(upper, x):
    return jnp.dot(x, upper.astype(F32), precision=lax.Precision.HIGHEST, preferred_element_type=F32)


def _prefix_max_rows(x):
    row = lax.broadcasted_iota(jnp.int32, x.shape, 0)
    s = 1
    while s < x.shape[0]:
        x = jnp.where(row >= s, jnp.maximum(x, pltpu.roll(x, s, 0)), x)
        s *= 2
    return x


def _state_spec(shape, layer):
    return pl.BlockSpec((None, 1) + shape, lambda i, j: (layer, i) + (0,) * len(shape))


def _state_out_spec(shape):
    return pl.BlockSpec((1,) + shape, lambda i, j: (i,) + (0,) * len(shape))


def _mlstm_body(q_ref, k_ref, v_ref, gc_ref, gr_ref, c0_ref, n0_ref, m0_ref,
                h_ref, c_ref, n_ref, m_ref):
    @pl.when(pl.program_id(1) == 0)
    def _():
        c_ref[...] = c0_ref[...]
        n_ref[...] = n0_ref[...]
        m_ref[...] = m0_ref[...]

    t = q_ref.shape[0]
    scale = C_HEAD_DIM ** -0.5
    lower, upper = _tri_masks(t)
    gc = gc_ref[...]
    gr = gr_ref[...]
    b_c = pltpu.roll(_cumsum_cols(lower, _log_sigmoid(gc)), GATE_COLS - FG_OFF, 1)
    b_r = _cumsum_rows(upper, _log_sigmoid(gr))[FG_OFF:FG_OFF + C_HEADS, :]
    c_c = gc - b_c
    c_r = gr[IG_OFF:IG_OFF + C_HEADS, :] - b_r
    m8 = m_ref[0]
    eye = lax.broadcasted_iota(jnp.int32, m8.shape, 0) == lax.broadcasted_iota(jnp.int32, m8.shape, 1)
    ms_row = jnp.sum(jnp.where(eye, m8, 0.0), axis=0, keepdims=True)
    big_m = jnp.maximum(ms_row, _prefix_max_rows(c_c))
    gw_all = jnp.exp(ms_row - big_m)
    floor_all = jnp.exp(-(b_c + big_m))
    m_last = big_m[t - 1:t, :]
    wj_all = jnp.exp(c_c - m_last)
    decay_all = jnp.exp(ms_row - m_last)
    m_new_all = b_c[t - 1:t, :] + m_last
    for h in range(C_HEADS):
        hs = slice(h * C_HEAD_DIM, (h + 1) * C_HEAD_DIM)
        col = lambda a: a[:, h:h + 1]
        qf = q_ref[:, hs]
        kf = k_ref[:, hs] * scale
        vb = v_ref[:, hs].astype(BF16)
        q2, k2 = _split_bf16(qf), _split_bf16(kf)
        w = _split_dot(_nt_dot, q2, k2) * jnp.exp(jnp.where(lower, c_r[h:h + 1, :] - col(big_m), -jnp.inf))
        gw = col(gw_all)
        cs = c_ref[0, h]
        ns = n_ref[0, h:h + 1, :]
        w_hi, w_lo = _split_bf16(w)
        num = _dot(w_hi, vb) + _dot(w_lo, vb) + gw * _split_dot(_dot, q2, _split_bf16(cs))
        nq = jnp.sum(w, axis=1, keepdims=True) + gw * jnp.sum(qf * ns, axis=1, keepdims=True)
        h_ref[:, hs] = num * (1.0 / jnp.maximum(jnp.abs(nq), col(floor_all)))
        decay = col(decay_all)
        kw = kf * col(wj_all)
        kw_hi, kw_lo = _split_bf16(kw)
        c_ref[0, h] = decay * cs + _tn_dot(kw_hi, vb) + _tn_dot(kw_lo, vb)
        n_ref[0, h:h + 1, :] = decay * ns + jnp.sum(kw, axis=0, keepdims=True)
        m_ref[0, h:h + 1, :] = jnp.broadcast_to(col(m_new_all), (1, C_HEAD_DIM))


def mlstm(proj, gates_c, gates_r, c0, n0, m0, layer, *, batch, seq, t):
    nb = seq // t
    row = lambda i, j: i * nb + j
    rows = lambda width, col: pl.BlockSpec((t, width), lambda i, j: (row(i, j), col))
    shapes = (c0.shape[2:], n0.shape[2:], m0.shape[2:])
    return pl.pallas_call(
        _mlstm_body,
        out_shape=(jax.ShapeDtypeStruct((batch * seq, C_WIDTH), F32),)
        + tuple(jax.ShapeDtypeStruct((batch,) + s, F32) for s in shapes),
        grid=(batch, nb),
        in_specs=[
            rows(C_WIDTH, ODD_QC), rows(C_WIDTH, ODD_KC), rows(C_WIDTH, ODD_VC), rows(GATE_COLS, 0),
            pl.BlockSpec((None, N_GATES, t), lambda i, j: (row(i, j), 0, 0)),
        ] + [_state_spec(s, layer) for s in shapes],
        out_specs=(rows(C_WIDTH, 0),) + tuple(_state_out_spec(s) for s in shapes),
        compiler_params=_params(("arbitrary", "arbitrary")),
        name="mlstm",
    )(proj, proj, proj, gates_c, gates_r, c0, n0, m0)


def _ssd_body(xd_ref, bc_ref, gc_ref, gr_ref, wx_ref, wbc_ref, bx_ref, bbc_ref,
              alr_ref, alc_ref, dskip_ref, hx0_ref, hbc0_ref, s0_ref,
              y_ref, s_ref, hxo_ref, hbco_ref, hx, hbc):
    @pl.when(pl.program_id(1) == 0)
    def _():
        s_ref[...] = s0_ref[...]
        hx[...] = hx0_ref[0]
        hbc[...] = hbc0_ref[0]

    t = xd_ref.shape[0]
    xr = xd_ref[...]
    br = bc_ref[...]
    xs = _silu(_causal_dwconv(xr, hx[...], wx_ref, bx_ref))
    bcs = _silu(_causal_dwconv(br, hbc[...], wbc_ref, bbc_ref))
    hx[...] = xr[t - SUBLANES:t]
    hbc[...] = br[t - SUBLANES:t]
    hxo_ref[0] = xr[t - SUBLANES:t]
    hbco_ref[0] = br[t - SUBLANES:t]

    lower, upper = _tri_masks(t)
    dt_c = _softplus(gc_ref[:, DT_OFF:DT_OFF + D_HEADS])
    dt_r = _softplus(gr_ref[DT_OFF:DT_OFF + D_HEADS, :])
    a_c = _cumsum_cols(lower, dt_c * -jnp.exp(alr_ref[...]))
    a_r = _cumsum_rows(upper, dt_r * -jnp.exp(alc_ref[...]))
    for g in range(D_GROUPS):
        bmg = bcs[:, g * D_STATE:(g + 1) * D_STATE].astype(BF16)
        cmg = bcs[:, (D_GROUPS + g) * D_STATE:(D_GROUPS + g + 1) * D_STATE].astype(BF16)
        cb = _nt_dot(cmg, bmg)
        for hh in range(D_HG):
            h = g * D_HG + hh
            hs = slice(h * D_HEAD_DIM, (h + 1) * D_HEAD_DIM)
            acb = jnp.broadcast_to(a_c[:, h:h + 1], (t, max(t, D_HEAD_DIM)))
            ace = acb[:, :D_HEAD_DIM]
            lw = jnp.exp(jnp.where(lower, acb[:, :t] - a_r[h:h + 1, :], -jnp.inf))
            xh = xs[:, hs]
            sh = s_ref[0, h]
            y = _dot((cb * (lw * dt_r[h:h + 1, :])).astype(BF16), xh.astype(BF16))
            y = y + jnp.exp(ace) * _nt_dot(cmg, sh.astype(BF16))
            y_ref[:, hs] = y + dskip_ref[:, hs] * xh
            at = a_c[t - 1:t, h:h + 1]
            wj = jnp.exp(at - ace) * dt_c[:, h:h + 1]
            s_ref[0, h] = jnp.exp(at) * sh + _tn_dot((xh * wj).astype(BF16), bmg)


def ssd(proj, gates_c, gates_r, wx, wbc, bx, bbc, alog_r, alog_c, dskip, hx0, hbc0, s0, layer, state_layer,
        *, batch, seq, t):
    nb = seq // t
    row = lambda i, j: i * nb + j
    rows = lambda width, col: pl.BlockSpec((t, width), lambda i, j: (row(i, j), col))
    param = lambda a: _layer_param(a.shape[1:], layer)
    shapes = (s0.shape[2:], hx0.shape[2:], hbc0.shape[2:])
    return pl.pallas_call(
        _ssd_body,
        out_shape=(jax.ShapeDtypeStruct((batch * seq, D_WIDTH), F32),)
        + tuple(jax.ShapeDtypeStruct((batch,) + s, F32) for s in shapes),
        grid=(batch, nb),
        in_specs=[
            rows(D_WIDTH, ODD_XD), rows(D_BC, ODD_BC_BLOCK),
            rows(GATE_COLS, 0), pl.BlockSpec((None, N_GATES, t), lambda i, j: (row(i, j), 0, 0)),
            param(wx), param(wbc), param(bx), param(bbc), param(alog_r), param(alog_c), param(dskip),
            _state_spec(shapes[1], state_layer), _state_spec(shapes[2], state_layer), _state_spec(shapes[0], state_layer),
        ],
        out_specs=(rows(D_WIDTH, 0),) + tuple(_state_out_spec(s) for s in shapes),
        scratch_shapes=[pltpu.VMEM((SUBLANES, D_WIDTH), F32), pltpu.VMEM((SUBLANES, D_BC), F32)],
        compiler_params=_params(("arbitrary", "arbitrary")),
        name="ssd",
    )(proj, proj, gates_c, gates_r, wx, wbc, bx, bbc, alog_r, alog_c, dskip, hx0, hbc0, s0)


def _pad_state_rows(buf):
    pad = jnp.zeros(buf.shape[:-2] + (SUBLANES - buf.shape[-2], buf.shape[-1]), F32)
    return jnp.concatenate([pad, buf], axis=-2)


def _row_gates(gates_c, t):
    m = gates_c.shape[0]
    return gates_c[:, :N_GATES].reshape(m // t, t, N_GATES).transpose(0, 2, 1)


def _even_col_map(j):
    n_gate, n_head = MIX_WIDTH // 512, (3 * A_WIDTH + 3 * B_WIDTH) // 512
    return jnp.where(j < n_gate, j + n_head + 1, jnp.where(j == n_gate, n_head, j - n_gate - 1))


def _odd_weights(w_in, b_in):
    c4 = 4 * C_WIDTH
    g0 = c4 + 2 * C_HEADS
    d0 = g0 + D_XBC
    q0 = d0 + D_HEADS
    t0 = q0 + M_WIDTH
    big = lambda a: jnp.concatenate([a[..., t0:], a[..., q0:t0], a[..., :c4], a[..., g0:d0]], axis=-1)
    small = lambda a: jnp.concatenate(
        [a[..., c4:g0], a[..., d0:q0], jnp.zeros(a.shape[:-1] + (GATE_COLS - N_GATES,), a.dtype)], axis=-1)
    qk_lo = _split_bf16(w_in[..., :2 * C_WIDTH])[1]
    small_hi, small_lo = _split_bf16(small(w_in))
    return (big(w_in.astype(BF16)), big(b_in)[:, None, :], qk_lo), (small_hi, small(b_in)[:, None, :], small_lo)


def _heads(a, batch, seq, col, last):
    part = a.reshape(batch, seq, a.shape[1])[:, seq - last:, col * A_WIDTH:(col + 1) * A_WIDTH]
    return part.reshape(batch, last, A_HEADS, A_HEAD_DIM)


def kernel(x_prompt, x_sample, mem_prompt, cache_a_k, cache_a_v, cache_mem_k, cache_mem_v, state_b_conv, state_c_C, state_c_n, state_c_m, state_d_ssm, state_d_conv, w_in_even, b_in_even, w_in_odd, b_in_odd, w_out, w_mem_kv, ln_g, ln_b, rel_bias_table, b_conv_w, b_conv_b, d_conv_w, d_conv_b, d_A_log, d_skip, d_norm_w):
    bp, sp, _ = x_prompt.shape
    bs, ss, _ = x_sample.shape
    n_even, n_odd = w_in_even.shape[0], w_in_odd.shape[0]
    lc = cache_a_k.shape[2]
    xp = x_prompt.reshape(bp * sp, D_MODEL)
    xs = x_sample.reshape(bs * ss, D_MODEL)
    xp16, xs16 = xp.astype(BF16), xs.astype(BF16)
    mem = mem_prompt.reshape(bp * MEM_LEN, D_MODEL).astype(BF16)

    we, be = w_in_even.astype(BF16), b_in_even[:, None, :]
    (wo_big, bo_big, wo_qk_lo), (wo_small, bo_small, wo_small_lo) = _odd_weights(w_in_odd, b_in_odd)
    qk_blocks = (ODD_QC * C_WIDTH // 512, (ODD_KC + 1) * C_WIDTH // 512)
    w_out_b = w_out.astype(BF16)
    w_mem_b = w_mem_kv.astype(BF16)
    zero_bias = jnp.zeros((1, 1, 2 * M_WIDTH), F32)
    ln_g3, ln_b3 = ln_g[:, None, :], ln_b[:, None, :]
    conv_b3 = b_conv_b[:, None, :]
    norm_w3 = d_norm_w[:, None, :]
    d_wx, d_wbc = d_conv_w[:, :, :D_WIDTH], d_conv_w[:, :, D_WIDTH:]
    d_bx, d_bbc = d_conv_b[:, None, :D_WIDTH], d_conv_b[:, None, D_WIDTH:]
    alog_r, alog_c = d_A_log[:, None, :], d_A_log[:, :, None]
    dskip = jnp.repeat(d_skip, D_HEAD_DIM, axis=-1)[:, None, :]

    s_halo_b = _pad_state_rows(state_b_conv)
    s_halo_d = _pad_state_rows(state_d_conv)
    s_hx0, s_hbc0 = s_halo_d[..., :D_WIDTH], s_halo_d[..., D_WIDTH:]
    s_m0 = jnp.broadcast_to(state_c_m[..., None], state_c_m.shape + (C_HEAD_DIM,))
    zeros = lambda *shape: jnp.zeros((1, bp) + shape, F32)
    p_halo_b, p_hx0, p_hbc0 = zeros(SUBLANES, B_WIDTH), zeros(SUBLANES, D_WIDTH), zeros(SUBLANES, D_BC)
    p_c0, p_n0, p_m0 = zeros(C_HEADS, C_HEAD_DIM, C_HEAD_DIM), zeros(C_HEADS, C_HEAD_DIM), zeros(C_HEADS, C_HEAD_DIM)
    p_s0 = zeros(D_HEADS, D_HEAD_DIM, D_STATE)

    outs = {k: [] for k in ("p_mk", "p_mv", "p_ak", "p_av", "p_bc", "p_cc", "p_cn", "p_cm", "p_ds", "p_dc",
                            "s_ak", "s_av", "s_bc", "s_cc", "s_cn", "s_cm", "s_ds", "s_dc")}
    tp = min(SCAN_BLOCK, sp)
    ts = min(SCAN_BLOCK, ss)
    for l in range(DEPTH):
        i = l // 2
        mkv = linear(mem, w_mem_b, zero_bias.repeat(DEPTH, 0), l)
        mkv5 = mkv.reshape(1, bp, MEM_LEN, 2 * M_HEADS, M_HEAD_DIM)
        mk5, mv5 = mkv5[:, :, :, :M_HEADS], mkv5[:, :, :, M_HEADS:]
        outs["p_mk"].append(mk5[0])
        outs["p_mv"].append(mv5[0])
        flat = lambda a: a.reshape(a.shape[0], a.shape[1], MEM_LEN * M_HEADS, M_HEAD_DIM)
        p_mem = (flat(mk5), flat(mv5), lambda b: (0, b, 0, 0), lambda b: (0, b, 0, 0))
        s_mem = (flat(cache_mem_k), flat(cache_mem_v), lambda b, l=l: (l, b, 0, 0), lambda b, l=l: (l, b, 0, 0))
        if l % 2 == 0:
            proj_p = linear(xp16, we, be, i, col_map=_even_col_map)
            proj_s = linear(xs16, we, be, i, col_map=_even_col_map)
            ya_p = band_attention_prompt(proj_p, rel_bias_band(rel_bias_table[i]), tq=A_WINDOW)
            bias_s = rel_bias(rel_bias_table[i], ss, lc + ss, lc)
            rows_of = lambda a: a.reshape(a.shape[0], a.shape[1], lc * A_HEADS, A_HEAD_DIM)
            ya_s = band_attention_sample(proj_s, rows_of(cache_a_k), rows_of(cache_a_v), i,
                                         bias_s[:, :, :lc], bias_s[:, :, lc:], t=ss)
            ym_p = mem_attention(proj_p, *p_mem, batch=bp, seq=sp, tl=min(sp, 512))
            ym_s = mem_attention(proj_s, *s_mem, batch=bs, seq=ss, tl=min(ss, 512))
            tail = (b_conv_w, conv_b3)
            xp_new, xp16_new, xplo_new, tail_p = even_tail(proj_p, ya_p, ym_p, xp, *tail, p_halo_b, 0, w_out_b, ln_g3, ln_b3,
                                                 i, l, seq=sp, tm=min(sp, 256))
            xs_new, xs16_new, xslo_new, tail_s = even_tail(proj_s, ya_s, ym_s, xs, *tail, s_halo_b, i, w_out_b, ln_g3, ln_b3,
                                                 i, l, seq=ss, tm=min(ss, 256))
            outs["p_ak"].append(_heads(proj_p, bp, sp, EVEN_KA, A_WINDOW))
            outs["p_av"].append(_heads(proj_p, bp, sp, EVEN_VA, A_WINDOW))
            outs["s_ak"].append(_heads(proj_s, bs, ss, EVEN_KA, ss))
            outs["s_av"].append(_heads(proj_s, bs, ss, EVEN_VA, ss))
            outs["p_bc"].append(tail_p[:, SUBLANES - (B_CONV - 1):])
            outs["s_bc"].append(tail_s[:, SUBLANES - (B_CONV - 1):])
        else:
            ssd_params = (d_wx, d_wbc, d_bx, d_bbc, alog_r, alog_c, dskip)
            res = []
            for x, x16, xlo, batch, seq, t, mem_args, states, sl in (
                    (xp, xp16, xplo, bp, sp, tp, p_mem, (p_c0, p_n0, p_m0, p_hx0, p_hbc0, p_s0), 0),
                    (xs, xs16, xslo, bs, ss, ts, s_mem, (state_c_C, state_c_n, s_m0, s_hx0, s_hbc0, state_d_ssm), i)):
                c0, n0, m0, hx0, hbc0, s0 = states
                proj = linear(x16, wo_big, bo_big, i, fine=(xlo, wo_qk_lo, *qk_blocks) if i == 0 else None)
                gates_c = linear(x16, wo_small, bo_small, i, tn=GATE_COLS, fine=(xlo, wo_small_lo, 0, 1))
                gates_r = _row_gates(gates_c, t)
                hc, cc, cn, cm = mlstm(proj, gates_c, gates_r, c0, n0, m0, sl, batch=batch, seq=seq, t=t)
                yd, s_new, hx, hbc = ssd(proj, gates_c, gates_r, *ssd_params, hx0, hbc0, s0, i, sl,
                                         batch=batch, seq=seq, t=t)
                ym = mem_attention(proj, *mem_args, batch=batch, seq=seq, tl=min(seq, 512))
                x_new, x16_new = odd_tail(proj, hc, yd, ym, x, norm_w3, i, w_out_b, ln_g3, ln_b3, l, tm=256)
                dbuf = jnp.concatenate([hx, hbc], axis=-1)[:, SUBLANES - (D_CONV - 1):]
                res.append((x_new, x16_new, cc, cn, cm[:, :, 0], s_new, dbuf))
            (xp_new, xp16_new, *st_p), (xs_new, xs16_new, *st_s) = res
            for key, val in zip(("p_cc", "p_cn", "p_cm", "p_ds", "p_dc"), st_p):
                outs[key].append(val)
            for key, val in zip(("s_cc", "s_cn", "s_cm", "s_ds", "s_dc"), st_s):
                outs[key].append(val)
        xp, xs, xp16, xs16 = xp_new, xs_new, xp16_new, xs16_new
        if l % 2 == 0:
            xplo, xslo = xplo_new, xslo_new
    st = jnp.stack
    return (xp.reshape(bp, sp, D_MODEL), xs.reshape(bs, ss, D_MODEL),
            st(outs["p_mk"]), st(outs["p_mv"]), st(outs["p_ak"]), st(outs["p_av"]), st(outs["p_bc"]),
            st(outs["p_cc"]), st(outs["p_cn"]), st(outs["p_cm"]), st(outs["p_ds"]), st(outs["p_dc"]),
            st(outs["s_ak"]), st(outs["s_av"]), st(outs["s_bc"]),
            st(outs["s_cc"]), st(outs["s_cn"]), st(outs["s_cm"]), st(outs["s_ds"]), st(outs["s_dc"]))
```

```python
import functools

import jax
import jax.numpy as jnp
from jax import lax
from jax.experimental import pallas as pl
from jax.experimental.pallas import tpu as pltpu

F32 = jnp.float32
BF16 = jnp.bfloat16

D_MODEL = 2048
DEPTH = 4
CHUNK = 64
ALPHA = (2 * DEPTH) ** 0.25
EPS = 1e-5
NEG = -1e30

A_WIDTH = D_MODEL // 2
A_HEAD_DIM = 128
A_HEADS = A_WIDTH // A_HEAD_DIM
A_BAND = 8
A_WINDOW = A_BAND * CHUNK
REL_CLIP = 128
B_WIDTH = D_MODEL // 2
B_CONV = 3
C_WIDTH = D_MODEL // 2
C_HEAD_DIM = 128
C_HEADS = C_WIDTH // C_HEAD_DIM
D_WIDTH = D_MODEL // 2
D_HEAD_DIM = 64
D_HEADS = D_WIDTH // D_HEAD_DIM
D_GROUPS = 2
D_HG = D_HEADS // D_GROUPS
D_STATE = 128
D_CONV = 4
D_BC = 2 * D_GROUPS * D_STATE
D_XBC = D_WIDTH + D_BC
MEM_LEN = 256
M_WIDTH = D_MODEL // 4
M_HEADS = 4
M_HEAD_DIM = M_WIDTH // M_HEADS
MIX_WIDTH = A_WIDTH + B_WIDTH + M_WIDTH

LANES = 128
SUBLANES = 8
VMEM_LIMIT_BYTES = 56 * 1024 * 1024

EVEN_QA, EVEN_KA, EVEN_VA, EVEN_BX, EVEN_BGATE, EVEN_CGATE = 3, 4, 5, 6, 7, 8
ODD_QC, ODD_KC, ODD_VC, ODD_OC, ODD_XD = 3, 4, 5, 6, 7
QM_BLOCK = MIX_WIDTH // M_WIDTH
ODD_BC_BLOCK = (MIX_WIDTH + M_WIDTH + 5 * 1024) // D_BC
GATE_COLS = LANES
IG_OFF, FG_OFF, DT_OFF = 0, C_HEADS, 2 * C_HEADS
N_GATES = 2 * C_HEADS + D_HEADS

SCAN_BLOCK = 256
Q_SUB = 256
K_SUB = Q_SUB + A_WINDOW
BAND_ROWS = (A_BAND + 1) * CHUNK
LOG2E = 1.4426950408889634
LINEAR_ROWS = 2048
TAIL_SUB = 128
MEM_GROUP = 4


def _params(semantics):
    return pltpu.CompilerParams(dimension_semantics=semantics, vmem_limit_bytes=VMEM_LIMIT_BYTES)


def _nt_dot(a, b):
    return lax.dot_general(a, b, (((1,), (1,)), ((), ())), preferred_element_type=F32)


def _tn_dot(a, b):
    return lax.dot_general(a, b, (((0,), (0,)), ((), ())), preferred_element_type=F32)


def _dot(a, b):
    return jnp.dot(a, b, preferred_element_type=F32)


def _split_dot(dot, a, b):
    return dot(a[0], b[0]) + (dot(a[0], b[1]) + dot(a[1], b[0]))


def _silu(x):
    return x * jax.nn.sigmoid(x)


def _softplus(x):
    return jnp.maximum(x, 0.0) + jnp.log1p(jnp.exp(-jnp.abs(x)))


def _log_sigmoid(x):
    return -_softplus(-x)


def _softmax_pv(s, v, exp=jnp.exp):
    m = jnp.max(s, axis=-1, keepdims=True)
    p = exp(s - m)
    inv = 1.0 / jnp.sum(p, axis=-1, keepdims=True)
    return _dot(p.astype(BF16), v) * inv


def _split_bf16(a):
    hi = a.astype(BF16)
    return hi, (a - hi.astype(F32)).astype(BF16)


def _linear_body(x_ref, w_ref, b_ref, o_ref):
    o_ref[...] = _dot(x_ref[...], w_ref[...]) + b_ref[...]


def _linear_fine_body(x_ref, w_ref, b_ref, xl_ref, wl_ref, o_ref, *, fine):
    j = pl.program_id(1)
    o_ref[...] = _dot(x_ref[...], w_ref[...]) + b_ref[...]

    @pl.when((j >= fine[0]) & (j < fine[1]))
    def _():
        o_ref[...] += _dot(xl_ref[...], w_ref[...]) + _dot(x_ref[...], wl_ref[...])


def linear(x, w, b, layer, *, tn=512, fine=None, col_map=lambda j: j):
    m, k = x.shape
    n = w.shape[2]
    tm = min(m, LINEAR_ROWS)
    assert m % tm == 0 and n % tn == 0, (m, n, tm, tn)
    in_specs = [
        pl.BlockSpec((tm, k), lambda i, j: (i, 0)),
        pl.BlockSpec((None, k, tn), lambda i, j: (layer, 0, col_map(j))),
        pl.BlockSpec((None, 1, tn), lambda i, j: (layer, 0, col_map(j))),
    ]
    body, args = _linear_body, (x, w, b)
    if fine is not None:
        x_lo, w_lo, j0, j1 = fine
        assert w_lo.shape[2] == (j1 - j0) * tn
        in_specs += [
            pl.BlockSpec((tm, k), lambda i, j: (i, 0)),
            pl.BlockSpec((None, k, tn), lambda i, j: (layer, 0, jnp.clip(j - j0, 0, j1 - j0 - 1))),
        ]
        body, args = functools.partial(_linear_fine_body, fine=(j0, j1)), (x, w, b, x_lo, w_lo)
    return pl.pallas_call(
        body,
        out_shape=jax.ShapeDtypeStruct((m, n), F32),
        grid=(m // tm, n // tn),
        in_specs=in_specs,
        out_specs=pl.BlockSpec((tm, tn), lambda i, j: (i, j)),
        compiler_params=_params(("parallel", "arbitrary")),
        name="linear",
    )(*args)


def _table_select(tab_ref, h, idx):
    def step(t, acc):
        return jnp.where(idx == t, tab_ref[h, t], acc)

    return lax.fori_loop(0, 2 * REL_CLIP + 1, step, jnp.zeros(idx.shape, F32))


def _relbias_body(tab_ref, o_ref, *, offset):
    ql, kl = o_ref.shape[1], o_ref.shape[2]
    qi = lax.broadcasted_iota(jnp.int32, (ql, kl), 0)
    ki = lax.broadcasted_iota(jnp.int32, (ql, kl), 1)
    idx = jnp.clip(qi + offset - ki, -REL_CLIP, REL_CLIP) + REL_CLIP
    o_ref[0] = _table_select(tab_ref, pl.program_id(0), idx)


def rel_bias(table, q_len, k_len, offset):
    heads = table.shape[0]
    return pl.pallas_call(
        functools.partial(_relbias_body, offset=offset),
        out_shape=jax.ShapeDtypeStruct((heads, q_len, k_len), F32),
        grid=(heads,),
        in_specs=[pl.BlockSpec(memory_space=pltpu.SMEM)],
        out_specs=pl.BlockSpec((1, q_len, k_len), lambda h: (h, 0, 0)),
        compiler_params=_params(("arbitrary",)),
        name="rel_bias",
    )(table)


def _relbias_band_body(tab_ref, o_ref):
    period = 2 * A_WINDOW
    u = lax.broadcasted_iota(jnp.int32, (SUBLANES, period), 1)
    d = jnp.where(u < K_SUB, u, u - period)
    idx = jnp.clip(A_WINDOW - d, -REL_CLIP, REL_CLIP) + REL_CLIP
    g = _table_select(tab_ref, pl.program_id(0), idx)
    full = pltpu.roll(jnp.broadcast_to(g[0:1], (Q_SUB, period)), 0, 1, stride=1, stride_axis=0)
    r = lax.broadcasted_iota(jnp.int32, (Q_SUB, K_SUB), 0)
    w = lax.broadcasted_iota(jnp.int32, (Q_SUB, K_SUB), 1)
    first = (r // CHUNK) * CHUNK
    in_band = (w >= first) & (w < first + BAND_ROWS)
    o_ref[0] = jnp.where(in_band, full[:, :K_SUB] * LOG2E, NEG)


def rel_bias_band(table):
    heads = table.shape[0]
    return pl.pallas_call(
        _relbias_band_body,
        out_shape=jax.ShapeDtypeStruct((heads, Q_SUB, K_SUB), F32),
        grid=(heads,),
        in_specs=[pl.BlockSpec(memory_space=pltpu.SMEM)],
        out_specs=pl.BlockSpec((1, Q_SUB, K_SUB), lambda h: (h, 0, 0)),
        compiler_params=_params(("arbitrary",)),
        name="rel_bias_band",
    )(table)


def _band_prompt_body(q_ref, k_ref, v_ref, bias_ref, o_ref, kwin, vwin):
    i = pl.program_id(0)
    tq = q_ref.shape[0]
    scale = A_HEAD_DIM ** -0.5 * LOG2E

    @pl.when(i == 0)
    def _():
        kwin[0:tq] = jnp.zeros((tq, A_WIDTH), BF16)
        vwin[0:tq] = jnp.zeros((tq, A_WIDTH), BF16)

    @pl.when(i > 0)
    def _():
        kwin[0:tq] = kwin[tq:2 * tq]
        vwin[0:tq] = vwin[tq:2 * tq]

    kwin[tq:2 * tq] = k_ref[...].astype(BF16)
    vwin[tq:2 * tq] = v_ref[...].astype(BF16)

    col = lax.broadcasted_iota(jnp.int32, (1, K_SUB), 1)
    unreal = jnp.where(i == 0, tq, 0)
    for r0 in range(0, tq, Q_SUB):
        neg = jnp.where(col + r0 < unreal, NEG, 0.0)
        for h in range(A_HEADS):
            hs = slice(h * A_HEAD_DIM, (h + 1) * A_HEAD_DIM)
            qh = (q_ref[r0:r0 + Q_SUB, hs] * scale).astype(BF16)
            s = _nt_dot(qh, kwin[r0:r0 + K_SUB, hs]) + bias_ref[h] + neg
            o_ref[r0:r0 + Q_SUB, hs] = _softmax_pv(s, vwin[r0:r0 + K_SUB, hs], jnp.exp2)


def band_attention_prompt(proj, bias, *, tq):
    s = proj.shape[0]
    assert tq >= A_WINDOW and tq % Q_SUB == 0 and Q_SUB % CHUNK == 0 and s % tq == 0
    return pl.pallas_call(
        _band_prompt_body,
        out_shape=jax.ShapeDtypeStruct((s, A_WIDTH), F32),
        grid=(s // tq,),
        in_specs=[
            pl.BlockSpec((tq, A_WIDTH), lambda i: (i, EVEN_QA)),
            pl.BlockSpec((tq, A_WIDTH), lambda i: (i, EVEN_KA)),
            pl.BlockSpec((tq, A_WIDTH), lambda i: (i, EVEN_VA)),
            pl.BlockSpec((A_HEADS, Q_SUB, K_SUB), lambda i: (0, 0, 0)),
        ],
        out_specs=pl.BlockSpec((tq, A_WIDTH), lambda i: (i, 0)),
        scratch_shapes=[pltpu.VMEM((2 * tq, A_WIDTH), BF16), pltpu.VMEM((2 * tq, A_WIDTH), BF16)],
        compiler_params=_params(("arbitrary",)),
        name="band_attention_prompt",
    )(proj, proj, proj, bias)


def _head_rows(ref, h, heads):
    return ref[pl.ds(h, ref.shape[0] // heads, stride=heads), :]


def _band_sample_body(q_ref, nk_ref, nv_ref, ck_ref, cv_ref, bc_ref, bn_ref, o_ref):
    scale = A_HEAD_DIM ** -0.5
    for h in range(A_HEADS):
        hs = slice(h * A_HEAD_DIM, (h + 1) * A_HEAD_DIM)
        qh = q_ref[:, hs].astype(BF16)
        s1 = _nt_dot(qh, _head_rows(ck_ref, h, A_HEADS).astype(BF16)) * scale + bc_ref[h]
        s2 = _nt_dot(qh, nk_ref[:, hs].astype(BF16)) * scale + bn_ref[h]
        m = jnp.maximum(jnp.max(s1, axis=-1, keepdims=True), jnp.max(s2, axis=-1, keepdims=True))
        p1 = jnp.exp(s1 - m)
        p2 = jnp.exp(s2 - m)
        inv = 1.0 / (jnp.sum(p1, axis=-1, keepdims=True) + jnp.sum(p2, axis=-1, keepdims=True))
        o = _dot(p1.astype(BF16), _head_rows(cv_ref, h, A_HEADS).astype(BF16))
        o = o + _dot(p2.astype(BF16), nv_ref[:, hs].astype(BF16))
        o_ref[:, hs] = o * inv


def band_attention_sample(proj, cache_k, cache_v, layer, bias_cache, bias_new, *, t):
    _, b, rows, _ = cache_k.shape
    lc = rows // A_HEADS
    cache = pl.BlockSpec((None, None, rows, A_HEAD_DIM), lambda i: (layer, i, 0, 0))
    return pl.pallas_call(
        _band_sample_body,
        out_shape=jax.ShapeDtypeStruct((b * t, A_WIDTH), F32),
        grid=(b,),
        in_specs=[
            pl.BlockSpec((t, A_WIDTH), lambda i: (i, EVEN_QA)),
            pl.BlockSpec((t, A_WIDTH), lambda i: (i, EVEN_KA)),
            pl.BlockSpec((t, A_WIDTH), lambda i: (i, EVEN_VA)),
            cache, cache,
            pl.BlockSpec((A_HEADS, t, lc), lambda i: (0, 0, 0)),
            pl.BlockSpec((A_HEADS, t, t), lambda i: (0, 0, 0)),
        ],
        out_specs=pl.BlockSpec((t, A_WIDTH), lambda i: (i, 0)),
        compiler_params=_params(("parallel",)),
        name="band_attention_sample",
    )(proj, proj, proj, cache_k, cache_v, bias_cache, bias_new)


def _mem_attn_body(q_ref, mk_ref, mv_ref, o_ref):
    scale = M_HEAD_DIM ** -0.5
    group = mk_ref.shape[0]
    tl = q_ref.shape[0] // group
    for g in range(group):
        rows = slice(g * tl, (g + 1) * tl)
        for h in range(M_HEADS):
            hs = slice(h * M_HEAD_DIM, (h + 1) * M_HEAD_DIM)
            s = _nt_dot(q_ref[rows, hs].astype(BF16), _head_rows(mk_ref.at[g], h, M_HEADS).astype(BF16)) * scale
            o_ref[rows, hs] = _softmax_pv(s, _head_rows(mv_ref.at[g], h, M_HEADS).astype(BF16))


def mem_attention(proj, mk, mv, kmap, vmap, *, batch, seq, tl):
    nt = seq // tl
    group = MEM_GROUP if (nt == 1 and batch % MEM_GROUP == 0) else 1
    mem = lambda f: pl.BlockSpec((None, group, MEM_LEN * M_HEADS, M_HEAD_DIM), lambda i, j: f(i))
    return pl.pallas_call(
        _mem_attn_body,
        out_shape=jax.ShapeDtypeStruct((batch * seq, M_WIDTH), F32),
        grid=(batch // group, nt),
        in_specs=[pl.BlockSpec((group * tl, M_WIDTH), lambda i, j: (i * nt + j, QM_BLOCK)), mem(kmap), mem(vmap)],
        out_specs=pl.BlockSpec((group * tl, M_WIDTH), lambda i, j: (i * nt + j, 0)),
        compiler_params=_params(("parallel", "parallel")),
        name="mem_attention",
    )(proj, mk, mv)


def _shift_rows(x, halo, s):
    rolled = pltpu.roll(x, s, 0)
    halo_rolled = pltpu.roll(halo, s, 0)
    row = lax.broadcasted_iota(jnp.int32, halo.shape, 0)
    top = jnp.where(row < s, halo_rolled, rolled[0:SUBLANES])
    if x.shape[0] == SUBLANES:
        return top
    return jnp.concatenate([top, rolled[SUBLANES:]], axis=0)


def _causal_dwconv(x, halo, w_ref, b_ref):
    width = w_ref.shape[0]
    y = b_ref[...]
    for j in range(width):
        s = width - 1 - j
        xs = x if s == 0 else _shift_rows(x, halo, s)
        y = y + xs * w_ref[j:j + 1, :]
    return y


def _residual_ln(acc, rows, x_ref, g_ref, b_ref, o_ref, o16_ref, olo_ref=None):
    y = ALPHA * x_ref[rows, :] + acc
    mu = jnp.mean(y, axis=-1, keepdims=True)
    d = y - mu
    var = jnp.mean(d * d, axis=-1, keepdims=True)
    out = d * lax.rsqrt(var + EPS) * g_ref[...] + b_ref[...]
    o_ref[rows, :] = out
    hi, lo = _split_bf16(out)
    o16_ref[rows, :] = hi
    if olo_ref is not None:
        olo_ref[rows, :] = lo


def _sub_blocks(tm):
    sub = min(TAIL_SUB, tm)
    return [slice(r, r + sub) for r in range(0, tm, sub)]


def _mix_matmul(parts, w_ref):
    acc, r0 = None, 0
    for p in parts:
        term = _dot(p.astype(BF16), w_ref[r0:r0 + p.shape[1], :])
        acc = term if acc is None else acc + term
        r0 += p.shape[1]
    return acc


def _even_tail_body(gate_ref, bx_ref, bg_ref, cg_ref, ya_ref, ym_ref, cw_ref, cb_ref, halo_ref,
                    x_ref, w_ref, g_ref, b_ref, o_ref, o16_ref, olo_ref, tail_ref, carry):
    @pl.when(pl.program_id(1) == 0)
    def _():
        carry[...] = halo_ref[...]

    tm = bx_ref.shape[0]
    conv_in = lambda rows: cg_ref[rows, :] * bx_ref[rows, :]
    for rows in _sub_blocks(tm):
        halo = carry[...] if rows.start == 0 else conv_in(slice(rows.start - SUBLANES, rows.start))
        yb = bg_ref[rows, :] * _causal_dwconv(conv_in(rows), halo, cw_ref, cb_ref)
        sg = _silu(gate_ref[rows, :])
        parts = (ya_ref[rows, :] * sg[:, 0:A_WIDTH],
                 yb * sg[:, A_WIDTH:A_WIDTH + B_WIDTH],
                 ym_ref[rows, :] * sg[:, A_WIDTH + B_WIDTH:MIX_WIDTH])
        _residual_ln(_mix_matmul(parts, w_ref), rows, x_ref, g_ref, b_ref, o_ref, o16_ref, olo_ref)
    last = conv_in(slice(tm - SUBLANES, tm))
    carry[...] = last
    tail_ref[...] = last


def _layer_param(shape, layer):
    return pl.BlockSpec((None,) + shape, lambda i, j: (layer,) + (0,) * len(shape))


def _out_weight(layer):
    return pl.BlockSpec((None, MIX_WIDTH, D_MODEL), lambda i, j: (layer, 0, 0), pipeline_mode=pl.Buffered(1))


def even_tail(proj, ya, ym, x, conv_w, conv_b, halo, halo_layer, w_out, ln_g, ln_b, conv_layer, layer, *, seq, tm):
    b = halo.shape[1]
    nt = seq // tm
    row = lambda i, j: i * nt + j
    rows = lambda width, col: pl.BlockSpec((tm, width), lambda i, j: (row(i, j), col))
    return pl.pallas_call(
        _even_tail_body,
        out_shape=(jax.ShapeDtypeStruct((b * seq, D_MODEL), F32),
                   jax.ShapeDtypeStruct((b * seq, D_MODEL), BF16),
                   jax.ShapeDtypeStruct((b * seq, D_MODEL), BF16),
                   jax.ShapeDtypeStruct((b, SUBLANES, B_WIDTH), F32)),
        grid=(b, nt),
        in_specs=[
            rows(MIX_WIDTH, 0), rows(B_WIDTH, EVEN_BX), rows(B_WIDTH, EVEN_BGATE), rows(B_WIDTH, EVEN_CGATE),
            rows(A_WIDTH, 0), rows(M_WIDTH, 0),
            _layer_param((B_CONV, B_WIDTH), conv_layer), _layer_param((1, B_WIDTH), conv_layer),
            pl.BlockSpec((None, None, SUBLANES, B_WIDTH), lambda i, j: (halo_layer, i, 0, 0)),
            rows(D_MODEL, 0), _out_weight(layer), _layer_param((1, D_MODEL), layer), _layer_param((1, D_MODEL), layer),
        ],
        out_specs=(rows(D_MODEL, 0), rows(D_MODEL, 0), rows(D_MODEL, 0),
                   pl.BlockSpec((None, SUBLANES, B_WIDTH), lambda i, j: (i, 0, 0))),
        scratch_shapes=[pltpu.VMEM((SUBLANES, B_WIDTH), F32)],
        compiler_params=_params(("arbitrary", "arbitrary")),
        name="even_tail",
    )(proj, proj, proj, proj, ya, ym, conv_w, conv_b, halo, x, w_out, ln_g, ln_b)


def _odd_tail_body(gate_ref, oc_ref, hc_ref, yd_ref, ym_ref, nw_ref, x_ref, w_ref, g_ref, b_ref, o_ref, o16_ref):
    for rows in _sub_blocks(x_ref.shape[0]):
        sg = _silu(gate_ref[rows, :])
        yc = hc_ref[rows, :] * jax.nn.sigmoid(oc_ref[rows, :]) * sg[:, 0:C_WIDTH]
        z = yd_ref[rows, :] * sg[:, C_WIDTH:C_WIDTH + D_WIDTH]
        yd = z * lax.rsqrt(jnp.mean(z * z, axis=-1, keepdims=True) + EPS) * nw_ref[...]
        parts = (yc, yd, ym_ref[rows, :] * sg[:, C_WIDTH + D_WIDTH:MIX_WIDTH])
        _residual_ln(_mix_matmul(parts, w_ref), rows, x_ref, g_ref, b_ref, o_ref, o16_ref)


def odd_tail(proj, hc, yd, ym, x, norm_w, norm_layer, w_out, ln_g, ln_b, layer, *, tm):
    m = x.shape[0]
    rows = lambda width, col: pl.BlockSpec((tm, width), lambda i, j: (i, col))
    return pl.pallas_call(
        _odd_tail_body,
        out_shape=(jax.ShapeDtypeStruct((m, D_MODEL), F32), jax.ShapeDtypeStruct((m, D_MODEL), BF16)),
        grid=(m // tm, 1),
        in_specs=[
            rows(MIX_WIDTH, 0), rows(C_WIDTH, ODD_OC), rows(C_WIDTH, 0), rows(D_WIDTH, 0), rows(M_WIDTH, 0),
            _layer_param((1, D_WIDTH), norm_layer),
            rows(D_MODEL, 0), _out_weight(layer), _layer_param((1, D_MODEL), layer), _layer_param((1, D_MODEL), layer),
        ],
        out_specs=(rows(D_MODEL, 0), rows(D_MODEL, 0)),
        compiler_params=_params(("parallel", "arbitrary")),
        name="odd_tail",
    )(proj, proj, hc, yd, ym, norm_w, x, w_out, ln_g, ln_b)


def _tri_masks(t):
    ri = lax.broadcasted_iota(jnp.int32, (t, t), 0)
    ci = lax.broadcasted_iota(jnp.int32, (t, t), 1)
    return ri >= ci, ri <= ci


def _cumsum_cols(lower, x):
    return jnp.dot(lower.astype(F32), x, precision=lax.Precision.HIGHEST, preferred_element_type=F32)


def _cumsum_rows(upper, x):
    return jnp.dot(x, upper.astype(F32), precision=lax.Precision.HIGHEST, preferred_element_type=F32)


def _prefix_max_rows(x):
    row = lax.broadcasted_iota(jnp.int32, x.shape, 0)
    s = 1
    while s < x.shape[0]:
        x = jnp.where(row >= s, jnp.maximum(x, pltpu.roll(x, s, 0)), x)
        s *= 2
    return x


def _state_spec(shape, layer):
    return pl.BlockSpec((None, 1) + shape, lambda i, j: (layer, i) + (0,) * len(shape))


def _state_out_spec(shape):
    return pl.BlockSpec((1,) + shape, lambda i, j: (i,) + (0,) * len(shape))


def _mlstm_body(q_ref, k_ref, v_ref, gc_ref, gr_ref, c0_ref, n0_ref, m0_ref,
                h_ref, c_ref, n_ref, m_ref):
    @pl.when(pl.program_id(1) == 0)
    def _():
        c_ref[...] = c0_ref[...]
        n_ref[...] = n0_ref[...]
        m_ref[...] = m0_ref[...]

    t = q_ref.shape[0]
    scale = C_HEAD_DIM ** -0.5
    lower, upper = _tri_masks(t)
    gc = gc_ref[...]
    gr = gr_ref[...]
    b_c = pltpu.roll(_cumsum_cols(lower, _log_sigmoid(gc)), GATE_COLS - FG_OFF, 1)
    b_r = _cumsum_rows(upper, _log_sigmoid(gr))[FG_OFF:FG_OFF + C_HEADS, :]
    c_c = gc - b_c
    c_r = gr[IG_OFF:IG_OFF + C_HEADS, :] - b_r
    m8 = m_ref[0]
    eye = lax.broadcasted_iota(jnp.int32, m8.shape, 0) == lax.broadcasted_iota(jnp.int32, m8.shape, 1)
    ms_row = jnp.sum(jnp.where(eye, m8, 0.0), axis=0, keepdims=True)
    big_m = jnp.maximum(ms_row, _prefix_max_rows(c_c))
    gw_all = jnp.exp(ms_row - big_m)
    floor_all = jnp.exp(-(b_c + big_m))
    m_last = big_m[t - 1:t, :]
    wj_all = jnp.exp(c_c - m_last)
    decay_all = jnp.exp(ms_row - m_last)
    m_new_all = b_c[t - 1:t, :] + m_last
    for h in range(C_HEADS):
        hs = slice(h * C_HEAD_DIM, (h + 1) * C_HEAD_DIM)
        col = lambda a: a[:, h:h + 1]
        qf = q_ref[:, hs]
        kf = k_ref[:, hs] * scale
        vb = v_ref[:, hs].astype(BF16)
        q2, k2 = _split_bf16(qf), _split_bf16(kf)
        w = _split_dot(_nt_dot, q2, k2) * jnp.exp(jnp.where(lower, c_r[h:h + 1, :] - col(big_m), -jnp.inf))
        gw = col(gw_all)
        cs = c_ref[0, h]
        ns = n_ref[0, h:h + 1, :]
        w_hi, w_lo = _split_bf16(w)
        num = _dot(w_hi, vb) + _dot(w_lo, vb) + gw * _split_dot(_dot, q2, _split_bf16(cs))
        nq = jnp.sum(w, axis=1, keepdims=True) + gw * jnp.sum(qf * ns, axis=1, keepdims=True)
        h_ref[:, hs] = num * (1.0 / jnp.maximum(jnp.abs(nq), col(floor_all)))
        decay = col(decay_all)
        kw = kf * col(wj_all)
        kw_hi, kw_lo = _split_bf16(kw)
        c_ref[0, h] = decay * cs + _tn_dot(kw_hi, vb) + _tn_dot(kw_lo, vb)
        n_ref[0, h:h + 1, :] = decay * ns + jnp.sum(kw, axis=0, keepdims=True)
        m_ref[0, h:h + 1, :] = jnp.broadcast_to(col(m_new_all), (1, C_HEAD_DIM))


def mlstm(proj, gates_c, gates_r, c0, n0, m0, layer, *, batch, seq, t):
    nb = seq // t
    row = lambda i, j: i * nb + j
    rows = lambda width, col: pl.BlockSpec((t, width), lambda i, j: (row(i, j), col))
    shapes = (c0.shape[2:], n0.shape[2:], m0.shape[2:])
    return pl.pallas_call(
        _mlstm_body,
        out_shape=(jax.ShapeDtypeStruct((batch * seq, C_WIDTH), F32),)
        + tuple(jax.ShapeDtypeStruct((batch,) + s, F32) for s in shapes),
        grid=(batch, nb),
        in_specs=[
            rows(C_WIDTH, ODD_QC), rows(C_WIDTH, ODD_KC), rows(C_WIDTH, ODD_VC), rows(GATE_COLS, 0),
            pl.BlockSpec((None, N_GATES, t), lambda i, j: (row(i, j), 0, 0)),
        ] + [_state_spec(s, layer) for s in shapes],
        out_specs=(rows(C_WIDTH, 0),) + tuple(_state_out_spec(s) for s in shapes),
        compiler_params=_params(("arbitrary", "arbitrary")),
        name="mlstm",
    )(proj, proj, proj, gates_c, gates_r, c0, n0, m0)


def _ssd_body(xd_ref, bc_ref, gc_ref, gr_ref, wx_ref, wbc_ref, bx_ref, bbc_ref,
              alr_ref, alc_ref, dskip_ref, hx0_ref, hbc0_ref, s0_ref,
              y_ref, s_ref, hxo_ref, hbco_ref, hx, hbc):
    @pl.when(pl.program_id(1) == 0)
    def _():
        s_ref[...] = s0_ref[...]
        hx[...] = hx0_ref[0]
        hbc[...] = hbc0_ref[0]

    t = xd_ref.shape[0]
    xr = xd_ref[...]
    br = bc_ref[...]
    xs = _silu(_causal_dwconv(xr, hx[...], wx_ref, bx_ref))
    bcs = _silu(_causal_dwconv(br, hbc[...], wbc_ref, bbc_ref))
    hx[...] = xr[t - SUBLANES:t]
    hbc[...] = br[t - SUBLANES:t]
    hxo_ref[0] = xr[t - SUBLANES:t]
    hbco_ref[0] = br[t - SUBLANES:t]

    lower, upper = _tri_masks(t)
    dt_c = _softplus(gc_ref[:, DT_OFF:DT_OFF + D_HEADS])
    dt_r = _softplus(gr_ref[DT_OFF:DT_OFF + D_HEADS, :])
    a_c = _cumsum_cols(lower, dt_c * -jnp.exp(alr_ref[...]))
    a_r = _cumsum_rows(upper, dt_r * -jnp.exp(alc_ref[...]))
    for g in range(D_GROUPS):
        bmg = bcs[:, g * D_STATE:(g + 1) * D_STATE].astype(BF16)
        cmg = bcs[:, (D_GROUPS + g) * D_STATE:(D_GROUPS + g + 1) * D_STATE].astype(BF16)
        cb = _nt_dot(cmg, bmg)
        for hh in range(D_HG):
            h = g * D_HG + hh
            hs = slice(h * D_HEAD_DIM, (h + 1) * D_HEAD_DIM)
            acb = jnp.broadcast_to(a_c[:, h:h + 1], (t, max(t, D_HEAD_DIM)))
            ace = acb[:, :D_HEAD_DIM]
            lw = jnp.exp(jnp.where(lower, acb[:, :t] - a_r[h:h + 1, :], -jnp.inf))
            xh = xs[:, hs]
            sh = s_ref[0, h]
            y = _dot((cb * (lw * dt_r[h:h + 1, :])).astype(BF16), xh.astype(BF16))
            y = y + jnp.exp(ace) * _nt_dot(cmg, sh.astype(BF16))
            y_ref[:, hs] = y + dskip_ref[:, hs] * xh
            at = a_c[t - 1:t, h:h + 1]
            wj = jnp.exp(at - ace) * dt_c[:, h:h + 1]
            s_ref[0, h] = jnp.exp(at) * sh + _tn_dot((xh * wj).astype(BF16), bmg)


def ssd(proj, gates_c, gates_r, wx, wbc, bx, bbc, alog_r, alog_c, dskip, hx0, hbc0, s0, layer, state_layer,
        *, batch, seq, t):
    nb = seq // t
    row = lambda i, j: i * nb + j
    rows = lambda width, col: pl.BlockSpec((t, width), lambda i, j: (row(i, j), col))
    param = lambda a: _layer_param(a.shape[1:], layer)
    shapes = (s0.shape[2:], hx0.shape[2:], hbc0.shape[2:])
    return pl.pallas_call(
        _ssd_body,
        out_shape=(jax.ShapeDtypeStruct((batch * seq, D_WIDTH), F32),)
        + tuple(jax.ShapeDtypeStruct((batch,) + s, F32) for s in shapes),
        grid=(batch, nb),
        in_specs=[
            rows(D_WIDTH, ODD_XD), rows(D_BC, ODD_BC_BLOCK),
            rows(GATE_COLS, 0), pl.BlockSpec((None, N_GATES, t), lambda i, j: (row(i, j), 0, 0)),
            param(wx), param(wbc), param(bx), param(bbc), param(alog_r), param(alog_c), param(dskip),
            _state_spec(shapes[1], state_layer), _state_spec(shapes[2], state_layer), _state_spec(shapes[0], state_layer),
        ],
        out_specs=(rows(D_WIDTH, 0),) + tuple(_state_out_spec(s) for s in shapes),
        scratch_shapes=[pltpu.VMEM((SUBLANES, D_WIDTH), F32), pltpu.VMEM((SUBLANES, D_BC), F32)],
        compiler_params=_params(("arbitrary", "arbitrary")),
        name="ssd",
    )(proj, proj, gates_c, gates_r, wx, wbc, bx, bbc, alog_r, alog_c, dskip, hx0, hbc0, s0)


def _pad_state_rows(buf):
    pad = jnp.zeros(buf.shape[:-2] + (SUBLANES - buf.shape[-2], buf.shape[-1]), F32)
    return jnp.concatenate([pad, buf], axis=-2)


def _row_gates(gates_c, t):
    m = gates_c.shape[0]
    return gates_c[:, :N_GATES].reshape(m // t, t, N_GATES).transpose(0, 2, 1)


def _even_col_map(j):
    n_gate, n_head = MIX_WIDTH // 512, (3 * A_WIDTH + 3 * B_WIDTH) // 512
    return jnp.where(j < n_gate, j + n_head + 1, jnp.where(j == n_gate, n_head, j - n_gate - 1))


ODD_C4 = 4 * C_WIDTH
ODD_XBC0 = ODD_C4 + 2 * C_HEADS
ODD_DT0 = ODD_XBC0 + D_XBC
ODD_QM0 = ODD_DT0 + D_HEADS
ODD_GATE0 = ODD_QM0 + M_WIDTH
ODD_BIG = MIX_WIDTH + M_WIDTH + ODD_C4 + D_XBC
PREP_COLS = 512


def _odd_src_col(c):
    n_gate, n_c4 = MIX_WIDTH // PREP_COLS, ODD_C4 // PREP_COLS
    return jnp.where(c < n_gate, ODD_GATE0 + PREP_COLS * c,
                     jnp.where(c == n_gate, ODD_QM0,
                               jnp.where(c <= n_gate + n_c4, PREP_COLS * (c - n_gate - 1),
                                         ODD_XBC0 + PREP_COLS * (c - n_gate - 1 - n_c4))))


def _prep_big_body(wt_ref, o_ref):
    o_ref[...] = wt_ref[0].T.astype(BF16)


def _prep_lo_body(wt_ref, o_ref):
    o_ref[...] = _split_bf16(wt_ref[...].T)[1]


def _prep_gates_body(a_ref, b_ref, hi_ref, lo_ref):
    pad = jnp.zeros((GATE_COLS - N_GATES, a_ref.shape[1]), F32)
    hi, lo = _split_bf16(jnp.concatenate([a_ref[...], b_ref[...], pad], axis=0).T)
    hi_ref[...] = hi
    lo_ref[...] = lo


def _odd_weights(w_in, b_in):
    layers, k, n = w_in.shape
    wt = jnp.swapaxes(w_in, 1, 2)
    out_block = pl.BlockSpec((None, k, PREP_COLS), lambda l, c: (l, 0, c))
    big_w = pl.pallas_call(
        _prep_big_body,
        out_shape=jax.ShapeDtypeStruct((layers, k, ODD_BIG), BF16),
        grid=(layers, ODD_BIG // PREP_COLS),
        in_specs=[pl.BlockSpec((pl.Element(1), pl.Element(PREP_COLS), pl.Element(k)),
                               lambda l, c: (l, pl.multiple_of(_odd_src_col(c), 2 * C_HEADS), 0))],
        out_specs=out_block,
        compiler_params=_params(("parallel", "parallel")),
        name="odd_weight_prep",
    )(wt)
    qk_lo = pl.pallas_call(
        _prep_lo_body,
        out_shape=jax.ShapeDtypeStruct((layers, k, 2 * C_WIDTH), BF16),
        grid=(layers, 2 * C_WIDTH // PREP_COLS),
        in_specs=[pl.BlockSpec((None, PREP_COLS, k), lambda l, c: (l, c, 0))],
        out_specs=out_block,
        compiler_params=_params(("parallel", "parallel")),
        name="odd_weight_prep_lo",
    )(wt)
    rows16 = lambda first: pl.BlockSpec((None, 2 * C_HEADS, k), lambda l: (l, first // (2 * C_HEADS), 0))
    gate_out = pl.BlockSpec((None, k, GATE_COLS), lambda l: (l, 0, 0))
    assert D_HEADS == 2 * C_HEADS and ODD_C4 % D_HEADS == 0 and ODD_DT0 % D_HEADS == 0
    small_hi, small_lo = pl.pallas_call(
        _prep_gates_body,
        out_shape=(jax.ShapeDtypeStruct((layers, k, GATE_COLS), BF16),) * 2,
        grid=(layers,),
        in_specs=[rows16(ODD_C4), rows16(ODD_DT0)],
        out_specs=(gate_out, gate_out),
        compiler_params=_params(("parallel",)),
        name="odd_weight_prep_gates",
    )(wt, wt)
    big = lambda a: jnp.concatenate(
        [a[..., ODD_GATE0:], a[..., ODD_QM0:ODD_GATE0], a[..., :ODD_C4], a[..., ODD_XBC0:ODD_DT0]], axis=-1)
    small = lambda a: jnp.concatenate(
        [a[..., ODD_C4:ODD_XBC0], a[..., ODD_DT0:ODD_QM0], jnp.zeros(a.shape[:-1] + (GATE_COLS - N_GATES,), a.dtype)],
        axis=-1)
    return (big_w, big(b_in)[:, None, :], qk_lo), (small_hi, small(b_in)[:, None, :], small_lo)


def _heads(a, batch, seq, col, last):
    part = a.reshape(batch, seq, a.shape[1])[:, seq - last:, col * A_WIDTH:(col + 1) * A_WIDTH]
    return part.reshape(batch, last, A_HEADS, A_HEAD_DIM)


def kernel(x_prompt, x_sample, mem_prompt, cache_a_k, cache_a_v, cache_mem_k, cache_mem_v, state_b_conv, state_c_C, state_c_n, state_c_m, state_d_ssm, state_d_conv, w_in_even, b_in_even, w_in_odd, b_in_odd, w_out, w_mem_kv, ln_g, ln_b, rel_bias_table, b_conv_w, b_conv_b, d_conv_w, d_conv_b, d_A_log, d_skip, d_norm_w):
    bp, sp, _ = x_prompt.shape
    bs, ss, _ = x_sample.shape
    n_even, n_odd = w_in_even.shape[0], w_in_odd.shape[0]
    lc = cache_a_k.shape[2]
    xp = x_prompt.reshape(bp * sp, D_MODEL)
    xs = x_sample.reshape(bs * ss, D_MODEL)
    xp16, xs16 = xp.astype(BF16), xs.astype(BF16)
    mem = mem_prompt.reshape(bp * MEM_LEN, D_MODEL).astype(BF16)

    we, be = w_in_even.astype(BF16), b_in_even[:, None, :]
    (wo_big, bo_big, wo_qk_lo), (wo_small, bo_small, wo_small_lo) = _odd_weights(w_in_odd, b_in_odd)
    qk_blocks = (ODD_QC * C_WIDTH // 512, (ODD_KC + 1) * C_WIDTH // 512)
    w_out_b = w_out.astype(BF16)
    w_mem_b = w_mem_kv.astype(BF16)
    zero_bias = jnp.zeros((1, 1, 2 * M_WIDTH), F32)
    ln_g3, ln_b3 = ln_g[:, None, :], ln_b[:, None, :]
    conv_b3 = b_conv_b[:, None, :]
    norm_w3 = d_norm_w[:, None, :]
    d_wx, d_wbc = d_conv_w[:, :, :D_WIDTH], d_conv_w[:, :, D_WIDTH:]
    d_bx, d_bbc = d_conv_b[:, None, :D_WIDTH], d_conv_b[:, None, D_WIDTH:]
    alog_r, alog_c = d_A_log[:, None, :], d_A_log[:, :, None]
    dskip = jnp.repeat(d_skip, D_HEAD_DIM, axis=-1)[:, None, :]

    s_halo_b = _pad_state_rows(state_b_conv)
    s_halo_d = _pad_state_rows(state_d_conv)
    s_hx0, s_hbc0 = s_halo_d[..., :D_WIDTH], s_halo_d[..., D_WIDTH:]
    s_m0 = jnp.broadcast_to(state_c_m[..., None], state_c_m.shape + (C_HEAD_DIM,))
    zeros = lambda *shape: jnp.zeros((1, bp) + shape, F32)
    p_halo_b, p_hx0, p_hbc0 = zeros(SUBLANES, B_WIDTH), zeros(SUBLANES, D_WIDTH), zeros(SUBLANES, D_BC)
    p_c0, p_n0, p_m0 = zeros(C_HEADS, C_HEAD_DIM, C_HEAD_DIM), zeros(C_HEADS, C_HEAD_DIM), zeros(C_HEADS, C_HEAD_DIM)
    p_s0 = zeros(D_HEADS, D_HEAD_DIM, D_STATE)

    outs = {k: [] for k in ("p_mk", "p_mv", "p_ak", "p_av", "p_bc", "p_cc", "p_cn", "p_cm", "p_ds", "p_dc",
                            "s_ak", "s_av", "s_bc", "s_cc", "s_cn", "s_cm", "s_ds", "s_dc")}
    tp = min(SCAN_BLOCK, sp)
    ts = min(SCAN_BLOCK, ss)
    for l in range(DEPTH):
        i = l // 2
        mkv = linear(mem, w_mem_b, zero_bias.repeat(DEPTH, 0), l)
        mkv5 = mkv.reshape(1, bp, MEM_LEN, 2 * M_HEADS, M_HEAD_DIM)
        mk5, mv5 = mkv5[:, :, :, :M_HEADS], mkv5[:, :, :, M_HEADS:]
        outs["p_mk"].append(mk5[0])
        outs["p_mv"].append(mv5[0])
        flat = lambda a: a.reshape(a.shape[0], a.shape[1], MEM_LEN * M_HEADS, M_HEAD_DIM)
        p_mem = (flat(mk5), flat(mv5), lambda b: (0, b, 0, 0), lambda b: (0, b, 0, 0))
        s_mem = (flat(cache_mem_k), flat(cache_mem_v), lambda b, l=l: (l, b, 0, 0), lambda b, l=l: (l, b, 0, 0))
        if l % 2 == 0:
            proj_p = linear(xp16, we, be, i, col_map=_even_col_map)
            proj_s = linear(xs16, we, be, i, col_map=_even_col_map)
            ya_p = band_attention_prompt(proj_p, rel_bias_band(rel_bias_table[i]), tq=A_WINDOW)
            bias_s = rel_bias(rel_bias_table[i], ss, lc + ss, lc)
            rows_of = lambda a: a.reshape(a.shape[0], a.shape[1], lc * A_HEADS, A_HEAD_DIM)
            ya_s = band_attention_sample(proj_s, rows_of(cache_a_k), rows_of(cache_a_v), i,
                                         bias_s[:, :, :lc], bias_s[:, :, lc:], t=ss)
            ym_p = mem_attention(proj_p, *p_mem, batch=bp, seq=sp, tl=min(sp, 512))
            ym_s = mem_attention(proj_s, *s_mem, batch=bs, seq=ss, tl=min(ss, 512))
            tail = (b_conv_w, conv_b3)
            xp_new, xp16_new, xplo_new, tail_p = even_tail(proj_p, ya_p, ym_p, xp, *tail, p_halo_b, 0, w_out_b, ln_g3, ln_b3,
                                                 i, l, seq=sp, tm=min(sp, 256))
            xs_new, xs16_new, xslo_new, tail_s = even_tail(proj_s, ya_s, ym_s, xs, *tail, s_halo_b, i, w_out_b, ln_g3, ln_b3,
                                                 i, l, seq=ss, tm=min(ss, 256))
            outs["p_ak"].append(_heads(proj_p, bp, sp, EVEN_KA, A_WINDOW))
            outs["p_av"].append(_heads(proj_p, bp, sp, EVEN_VA, A_WINDOW))
            outs["s_ak"].append(_heads(proj_s, bs, ss, EVEN_KA, ss))
            outs["s_av"].append(_heads(proj_s, bs, ss, EVEN_VA, ss))
            outs["p_bc"].append(tail_p[:, SUBLANES - (B_CONV - 1):])
            outs["s_bc"].append(tail_s[:, SUBLANES - (B_CONV - 1):])
        else:
            ssd_params = (d_wx, d_wbc, d_bx, d_bbc, alog_r, alog_c, dskip)
            res = []
            for x, x16, xlo, batch, seq, t, mem_args, states, sl in (
                    (xp, xp16, xplo, bp, sp, tp, p_mem, (p_c0, p_n0, p_m0, p_hx0, p_hbc0, p_s0), 0),
                    (xs, xs16, xslo, bs, ss, ts, s_mem, (state_c_C, state_c_n, s_m0, s_hx0, s_hbc0, state_d_ssm), i)):
                c0, n0, m0, hx0, hbc0, s0 = states
                proj = linear(x16, wo_big, bo_big, i, fine=(xlo, wo_qk_lo, *qk_blocks) if i == 0 else None)
                gates_c = linear(x16, wo_small, bo_small, i, tn=GATE_COLS, fine=(xlo, wo_small_lo, 0, 1))
                gates_r = _row_gates(gates_c, t)
                hc, cc, cn, cm = mlstm(proj, gates_c, gates_r, c0, n0, m0, sl, batch=batch, seq=seq, t=t)
                yd, s_new, hx, hbc = ssd(proj, gates_c, gates_r, *ssd_params, hx0, hbc0, s0, i, sl,
                                         batch=batch, seq=seq, t=t)
                ym = mem_attention(proj, *mem_args, batch=batch, seq=seq, tl=min(seq, 512))
                x_new, x16_new = odd_tail(proj, hc, yd, ym, x, norm_w3, i, w_out_b, ln_g3, ln_b3, l, tm=256)
                dbuf = jnp.concatenate([hx, hbc], axis=-1)[:, SUBLANES - (D_CONV - 1):]
                res.append((x_new, x16_new, cc, cn, cm[:, :, 0], s_new, dbuf))
            (xp_new, xp16_new, *st_p), (xs_new, xs16_new, *st_s) = res
            for key, val in zip(("p_cc", "p_cn", "p_cm", "p_ds", "p_dc"), st_p):
                outs[key].append(val)
            for key, val in zip(("s_cc", "s_cn", "s_cm", "s_ds", "s_dc"), st_s):
                outs[key].append(val)
        xp, xs, xp16, xs16 = xp_new, xs_new, xp16_new, xs16_new
        if l % 2 == 0:
            xplo, xslo = xplo_new, xslo_new
    st = jnp.stack
    return (xp.reshape(bp, sp, D_MODEL), xs.reshape(bs, ss, D_MODEL),
            st(outs["p_mk"]), st(outs["p_mv"]), st(outs["p_ak"]), st(outs["p_av"]), st(outs["p_bc"]),
            st(outs["p_cc"]), st(outs["p_cn"]), st(outs["p_cm"]), st(outs["p_ds"]), st(outs["p_dc"]),
            st(outs["s_ak"]), st(outs["s_av"]), st(outs["s_bc"]),
            st(outs["s_cc"]), st(outs["s_cn"]), st(outs["s_cm"]), st(outs["s_ds"]), st(outs["s_dc"]))
```

```python
import functools

import jax
import jax.numpy as jnp
from jax import lax
from jax.experimental import pallas as pl
from jax.experimental.pallas import tpu as pltpu

F32 = jnp.float32
BF16 = jnp.bfloat16

D_MODEL = 2048
DEPTH = 4
CHUNK = 64
ALPHA = (2 * DEPTH) ** 0.25
EPS = 1e-5
NEG = -1e30

A_WIDTH = D_MODEL // 2
A_HEAD_DIM = 128
A_HEADS = A_WIDTH // A_HEAD_DIM
A_BAND = 8
A_WINDOW = A_BAND * CHUNK
REL_CLIP = 128
B_WIDTH = D_MODEL // 2
B_CONV = 3
C_WIDTH = D_MODEL // 2
C_HEAD_DIM = 128
C_HEADS = C_WIDTH // C_HEAD_DIM
D_WIDTH = D_MODEL // 2
D_HEAD_DIM = 64
D_HEADS = D_WIDTH // D_HEAD_DIM
D_GROUPS = 2
D_HG = D_HEADS // D_GROUPS
D_STATE = 128
D_CONV = 4
D_BC = 2 * D_GROUPS * D_STATE
D_XBC = D_WIDTH + D_BC
MEM_LEN = 256
M_WIDTH = D_MODEL // 4
M_HEADS = 4
M_HEAD_DIM = M_WIDTH // M_HEADS
MIX_WIDTH = A_WIDTH + B_WIDTH + M_WIDTH

LANES = 128
SUBLANES = 8
VMEM_LIMIT_BYTES = 56 * 1024 * 1024

EVEN_QA, EVEN_KA, EVEN_VA, EVEN_BX, EVEN_BGATE, EVEN_CGATE = 3, 4, 5, 6, 7, 8
ODD_QC, ODD_KC, ODD_VC, ODD_OC, ODD_XD = 3, 4, 5, 6, 7
QM_BLOCK = MIX_WIDTH // M_WIDTH
ODD_BC_BLOCK = (MIX_WIDTH + M_WIDTH + 5 * 1024) // D_BC
GATE_COLS = LANES
IG_OFF, FG_OFF, DT_OFF = 0, C_HEADS, 2 * C_HEADS
N_GATES = 2 * C_HEADS + D_HEADS

SCAN_BLOCK = 256
Q_SUB = 256
K_SUB = Q_SUB + A_WINDOW
BAND_ROWS = (A_BAND + 1) * CHUNK
LOG2E = 1.4426950408889634
LINEAR_ROWS = 2048
TAIL_SUB = 128
MEM_GROUP = 4


def _params(semantics):
    return pltpu.CompilerParams(dimension_semantics=semantics, vmem_limit_bytes=VMEM_LIMIT_BYTES)


def _nt_dot(a, b):
    return lax.dot_general(a, b, (((1,), (1,)), ((), ())), preferred_element_type=F32)


def _tn_dot(a, b):
    return lax.dot_general(a, b, (((0,), (0,)), ((), ())), preferred_element_type=F32)


def _dot(a, b):
    return jnp.dot(a, b, preferred_element_type=F32)


def _split_dot(dot, a, b):
    return dot(a[0], b[0]) + (dot(a[0], b[1]) + dot(a[1], b[0]))


def _silu(x):
    return x * jax.nn.sigmoid(x)


def _softplus(x):
    return jnp.maximum(x, 0.0) + jnp.log1p(jnp.exp(-jnp.abs(x)))


def _log_sigmoid(x):
    return -_softplus(-x)


def _softmax_pv(s, v, exp=jnp.exp):
    m = jnp.max(s, axis=-1, keepdims=True)
    p = exp(s - m)
    inv = 1.0 / jnp.sum(p, axis=-1, keepdims=True)
    return _dot(p.astype(BF16), v) * inv


def _split_bf16(a):
    hi = a.astype(BF16)
    return hi, (a - hi.astype(F32)).astype(BF16)


def _linear_body(x_ref, w_ref, b_ref, o_ref):
    o_ref[...] = _dot(x_ref[...], w_ref[...]) + b_ref[...]


def _linear_fine_body(x_ref, w_ref, b_ref, xl_ref, wl_ref, o_ref, *, fine):
    j = pl.program_id(1)
    o_ref[...] = _dot(x_ref[...], w_ref[...]) + b_ref[...]

    @pl.when((j >= fine[0]) & (j < fine[1]))
    def _():
        o_ref[...] += _dot(xl_ref[...], w_ref[...]) + _dot(x_ref[...], wl_ref[...])


def linear(x, w, b, layer, *, tn=512, fine=None, col_map=lambda j: j):
    m, k = x.shape
    n = w.shape[2]
    tm = min(m, LINEAR_ROWS)
    assert m % tm == 0 and n % tn == 0, (m, n, tm, tn)
    in_specs = [
        pl.BlockSpec((tm, k), lambda i, j: (i, 0)),
        pl.BlockSpec((None, k, tn), lambda i, j: (layer, 0, col_map(j))),
        pl.BlockSpec((None, 1, tn), lambda i, j: (layer, 0, col_map(j))),
    ]
    body, args = _linear_body, (x, w, b)
    if fine is not None:
        x_lo, w_lo, j0, j1 = fine
        assert w_lo.shape[2] == (j1 - j0) * tn
        in_specs += [
            pl.BlockSpec((tm, k), lambda i, j: (i, 0)),
            pl.BlockSpec((None, k, tn), lambda i, j: (layer, 0, jnp.clip(j - j0, 0, j1 - j0 - 1))),
        ]
        body, args = functools.partial(_linear_fine_body, fine=(j0, j1)), (x, w, b, x_lo, w_lo)
    return pl.pallas_call(
        body,
        out_shape=jax.ShapeDtypeStruct((m, n), F32),
        grid=(m // tm, n // tn),
        in_specs=in_specs,
        out_specs=pl.BlockSpec((tm, tn), lambda i, j: (i, j)),
        compiler_params=_params(("parallel", "arbitrary")),
        name="linear",
    )(*args)


def _table_select(tab_ref, h, idx):
    def step(t, acc):
        return jnp.where(idx == t, tab_ref[h, t], acc)

    return lax.fori_loop(0, 2 * REL_CLIP + 1, step, jnp.zeros(idx.shape, F32))


def _relbias_body(tab_ref, o_ref, *, offset):
    ql, kl = o_ref.shape[1], o_ref.shape[2]
    qi = lax.broadcasted_iota(jnp.int32, (ql, kl), 0)
    ki = lax.broadcasted_iota(jnp.int32, (ql, kl), 1)
    idx = jnp.clip(qi + offset - ki, -REL_CLIP, REL_CLIP) + REL_CLIP
    o_ref[0] = _table_select(tab_ref, pl.program_id(0), idx)


def rel_bias(table, q_len, k_len, offset):
    heads = table.shape[0]
    return pl.pallas_call(
        functools.partial(_relbias_body, offset=offset),
        out_shape=jax.ShapeDtypeStruct((heads, q_len, k_len), F32),
        grid=(heads,),
        in_specs=[pl.BlockSpec(memory_space=pltpu.SMEM)],
        out_specs=pl.BlockSpec((1, q_len, k_len), lambda h: (h, 0, 0)),
        compiler_params=_params(("arbitrary",)),
        name="rel_bias",
    )(table)


def _relbias_band_body(tab_ref, o_ref):
    period = 2 * A_WINDOW
    u = lax.broadcasted_iota(jnp.int32, (SUBLANES, period), 1)
    d = jnp.where(u < K_SUB, u, u - period)
    idx = jnp.clip(A_WINDOW - d, -REL_CLIP, REL_CLIP) + REL_CLIP
    g = _table_select(tab_ref, pl.program_id(0), idx)
    full = pltpu.roll(jnp.broadcast_to(g[0:1], (Q_SUB, period)), 0, 1, stride=1, stride_axis=0)
    r = lax.broadcasted_iota(jnp.int32, (Q_SUB, K_SUB), 0)
    w = lax.broadcasted_iota(jnp.int32, (Q_SUB, K_SUB), 1)
    first = (r // CHUNK) * CHUNK
    in_band = (w >= first) & (w < first + BAND_ROWS)
    o_ref[0] = jnp.where(in_band, full[:, :K_SUB] * LOG2E, NEG)


def rel_bias_band(table):
    heads = table.shape[0]
    return pl.pallas_call(
        _relbias_band_body,
        out_shape=jax.ShapeDtypeStruct((heads, Q_SUB, K_SUB), F32),
        grid=(heads,),
        in_specs=[pl.BlockSpec(memory_space=pltpu.SMEM)],
        out_specs=pl.BlockSpec((1, Q_SUB, K_SUB), lambda h: (h, 0, 0)),
        compiler_params=_params(("arbitrary",)),
        name="rel_bias_band",
    )(table)


def _band_prompt_body(q_ref, k_ref, v_ref, bias_ref, o_ref, kwin, vwin):
    i = pl.program_id(0)
    tq = q_ref.shape[0]
    scale = A_HEAD_DIM ** -0.5 * LOG2E

    @pl.when(i == 0)
    def _():
        kwin[0:tq] = jnp.zeros((tq, A_WIDTH), BF16)
        vwin[0:tq] = jnp.zeros((tq, A_WIDTH), BF16)

    @pl.when(i > 0)
    def _():
        kwin[0:tq] = kwin[tq:2 * tq]
        vwin[0:tq] = vwin[tq:2 * tq]

    kwin[tq:2 * tq] = k_ref[...].astype(BF16)
    vwin[tq:2 * tq] = v_ref[...].astype(BF16)

    col = lax.broadcasted_iota(jnp.int32, (1, K_SUB), 1)
    unreal = jnp.where(i == 0, tq, 0)
    for r0 in range(0, tq, Q_SUB):
        neg = jnp.where(col + r0 < unreal, NEG, 0.0)
        for h in range(A_HEADS):
            hs = slice(h * A_HEAD_DIM, (h + 1) * A_HEAD_DIM)
            qh = (q_ref[r0:r0 + Q_SUB, hs] * scale).astype(BF16)
            s = _nt_dot(qh, kwin[r0:r0 + K_SUB, hs]) + bias_ref[h] + neg
            o_ref[r0:r0 + Q_SUB, hs] = _softmax_pv(s, vwin[r0:r0 + K_SUB, hs], jnp.exp2)


def band_attention_prompt(proj, bias, *, tq):
    s = proj.shape[0]
    assert tq >= A_WINDOW and tq % Q_SUB == 0 and Q_SUB % CHUNK == 0 and s % tq == 0
    return pl.pallas_call(
        _band_prompt_body,
        out_shape=jax.ShapeDtypeStruct((s, A_WIDTH), F32),
        grid=(s // tq,),
        in_specs=[
            pl.BlockSpec((tq, A_WIDTH), lambda i: (i, EVEN_QA)),
            pl.BlockSpec((tq, A_WIDTH), lambda i: (i, EVEN_KA)),
            pl.BlockSpec((tq, A_WIDTH), lambda i: (i, EVEN_VA)),
            pl.BlockSpec((A_HEADS, Q_SUB, K_SUB), lambda i: (0, 0, 0)),
        ],
        out_specs=pl.BlockSpec((tq, A_WIDTH), lambda i: (i, 0)),
        scratch_shapes=[pltpu.VMEM((2 * tq, A_WIDTH), BF16), pltpu.VMEM((2 * tq, A_WIDTH), BF16)],
        compiler_params=_params(("arbitrary",)),
        name="band_attention_prompt",
    )(proj, proj, proj, bias)


def _head_rows(ref, h, heads):
    return ref[pl.ds(h, ref.shape[0] // heads, stride=heads), :]


def _band_sample_body(q_ref, nk_ref, nv_ref, ck_ref, cv_ref, bc_ref, bn_ref, o_ref):
    scale = A_HEAD_DIM ** -0.5
    for h in range(A_HEADS):
        hs = slice(h * A_HEAD_DIM, (h + 1) * A_HEAD_DIM)
        qh = q_ref[:, hs].astype(BF16)
        s1 = _nt_dot(qh, _head_rows(ck_ref, h, A_HEADS).astype(BF16)) * scale + bc_ref[h]
        s2 = _nt_dot(qh, nk_ref[:, hs].astype(BF16)) * scale + bn_ref[h]
        m = jnp.maximum(jnp.max(s1, axis=-1, keepdims=True), jnp.max(s2, axis=-1, keepdims=True))
        p1 = jnp.exp(s1 - m)
        p2 = jnp.exp(s2 - m)
        inv = 1.0 / (jnp.sum(p1, axis=-1, keepdims=True) + jnp.sum(p2, axis=-1, keepdims=True))
        o = _dot(p1.astype(BF16), _head_rows(cv_ref, h, A_HEADS).astype(BF16))
        o = o + _dot(p2.astype(BF16), nv_ref[:, hs].astype(BF16))
        o_ref[:, hs] = o * inv


def band_attention_sample(proj, cache_k, cache_v, layer, bias_cache, bias_new, *, t):
    _, b, rows, _ = cache_k.shape
    lc = rows // A_HEADS
    cache = pl.BlockSpec((None, None, rows, A_HEAD_DIM), lambda i: (layer, i, 0, 0))
    return pl.pallas_call(
        _band_sample_body,
        out_shape=jax.ShapeDtypeStruct((b * t, A_WIDTH), F32),
        grid=(b,),
        in_specs=[
            pl.BlockSpec((t, A_WIDTH), lambda i: (i, EVEN_QA)),
            pl.BlockSpec((t, A_WIDTH), lambda i: (i, EVEN_KA)),
            pl.BlockSpec((t, A_WIDTH), lambda i: (i, EVEN_VA)),
            cache, cache,
            pl.BlockSpec((A_HEADS, t, lc), lambda i: (0, 0, 0)),
            pl.BlockSpec((A_HEADS, t, t), lambda i: (0, 0, 0)),
        ],
        out_specs=pl.BlockSpec((t, A_WIDTH), lambda i: (i, 0)),
        compiler_params=_params(("parallel",)),
        name="band_attention_sample",
    )(proj, proj, proj, cache_k, cache_v, bias_cache, bias_new)


def _mem_attn_body(q_ref, mk_ref, mv_ref, o_ref):
    scale = M_HEAD_DIM ** -0.5
    group = mk_ref.shape[0]
    tl = q_ref.shape[0] // group
    for g in range(group):
        rows = slice(g * tl, (g + 1) * tl)
        for h in range(M_HEADS):
            hs = slice(h * M_HEAD_DIM, (h + 1) * M_HEAD_DIM)
            s = _nt_dot(q_ref[rows, hs].astype(BF16), _head_rows(mk_ref.at[g], h, M_HEADS).astype(BF16)) * scale
            o_ref[rows, hs] = _softmax_pv(s, _head_rows(mv_ref.at[g], h, M_HEADS).astype(BF16))


def mem_attention(proj, mk, mv, kmap, vmap, *, batch, seq, tl):
    nt = seq // tl
    group = MEM_GROUP if (nt == 1 and batch % MEM_GROUP == 0) else 1
    mem = lambda f: pl.BlockSpec((None, group, MEM_LEN * M_HEADS, M_HEAD_DIM), lambda i, j: f(i))
    return pl.pallas_call(
        _mem_attn_body,
        out_shape=jax.ShapeDtypeStruct((batch * seq, M_WIDTH), F32),
        grid=(batch // group, nt),
        in_specs=[pl.BlockSpec((group * tl, M_WIDTH), lambda i, j: (i * nt + j, QM_BLOCK)), mem(kmap), mem(vmap)],
        out_specs=pl.BlockSpec((group * tl, M_WIDTH), lambda i, j: (i * nt + j, 0)),
        compiler_params=_params(("parallel", "parallel")),
        name="mem_attention",
    )(proj, mk, mv)


def _shift_rows(x, halo, s):
    rolled = pltpu.roll(x, s, 0)
    halo_rolled = pltpu.roll(halo, s, 0)
    row = lax.broadcasted_iota(jnp.int32, halo.shape, 0)
    top = jnp.where(row < s, halo_rolled, rolled[0:SUBLANES])
    if x.shape[0] == SUBLANES:
        return top
    return jnp.concatenate([top, rolled[SUBLANES:]], axis=0)


def _causal_dwconv(x, halo, w_ref, b_ref):
    width = w_ref.shape[0]
    y = b_ref[...]
    for j in range(width):
        s = width - 1 - j
        xs = x if s == 0 else _shift_rows(x, halo, s)
        y = y + xs * w_ref[j:j + 1, :]
    return y


def _residual_ln(acc, rows, x_ref, g_ref, b_ref, o_ref, o16_ref, olo_ref=None):
    y = ALPHA * x_ref[rows, :] + acc
    mu = jnp.mean(y, axis=-1, keepdims=True)
    d = y - mu
    var = jnp.mean(d * d, axis=-1, keepdims=True)
    out = d * lax.rsqrt(var + EPS) * g_ref[...] + b_ref[...]
    o_ref[rows, :] = out
    hi, lo = _split_bf16(out)
    o16_ref[rows, :] = hi
    if olo_ref is not None:
        olo_ref[rows, :] = lo


def _sub_blocks(tm):
    sub = min(TAIL_SUB, tm)
    return [slice(r, r + sub) for r in range(0, tm, sub)]


def _mix_matmul(parts, w_ref):
    acc, r0 = None, 0
    for p in parts:
        term = _dot(p.astype(BF16), w_ref[r0:r0 + p.shape[1], :])
        acc = term if acc is None else acc + term
        r0 += p.shape[1]
    return acc


def _even_tail_body(gate_ref, bx_ref, bg_ref, cg_ref, ya_ref, ym_ref, cw_ref, cb_ref, halo_ref,
                    x_ref, w_ref, g_ref, b_ref, o_ref, o16_ref, olo_ref, tail_ref, carry):
    @pl.when(pl.program_id(1) == 0)
    def _():
        carry[...] = halo_ref[...]

    tm = bx_ref.shape[0]
    conv_in = lambda rows: cg_ref[rows, :] * bx_ref[rows, :]
    for rows in _sub_blocks(tm):
        halo = carry[...] if rows.start == 0 else conv_in(slice(rows.start - SUBLANES, rows.start))
        yb = bg_ref[rows, :] * _causal_dwconv(conv_in(rows), halo, cw_ref, cb_ref)
        sg = _silu(gate_ref[rows, :])
        parts = (ya_ref[rows, :] * sg[:, 0:A_WIDTH],
                 yb * sg[:, A_WIDTH:A_WIDTH + B_WIDTH],
                 ym_ref[rows, :] * sg[:, A_WIDTH + B_WIDTH:MIX_WIDTH])
        _residual_ln(_mix_matmul(parts, w_ref), rows, x_ref, g_ref, b_ref, o_ref, o16_ref, olo_ref)
    last = conv_in(slice(tm - SUBLANES, tm))
    carry[...] = last
    tail_ref[...] = last


def _layer_param(shape, layer):
    return pl.BlockSpec((None,) + shape, lambda i, j: (layer,) + (0,) * len(shape))


def _out_weight(layer):
    return pl.BlockSpec((None, MIX_WIDTH, D_MODEL), lambda i, j: (layer, 0, 0), pipeline_mode=pl.Buffered(1))


def even_tail(proj, ya, ym, x, conv_w, conv_b, halo, halo_layer, w_out, ln_g, ln_b, conv_layer, layer, *, seq, tm):
    b = halo.shape[1]
    nt = seq // tm
    row = lambda i, j: i * nt + j
    rows = lambda width, col: pl.BlockSpec((tm, width), lambda i, j: (row(i, j), col))
    return pl.pallas_call(
        _even_tail_body,
        out_shape=(jax.ShapeDtypeStruct((b * seq, D_MODEL), F32),
                   jax.ShapeDtypeStruct((b * seq, D_MODEL), BF16),
                   jax.ShapeDtypeStruct((b * seq, D_MODEL), BF16),
                   jax.ShapeDtypeStruct((b, SUBLANES, B_WIDTH), F32)),
        grid=(b, nt),
        in_specs=[
            rows(MIX_WIDTH, 0), rows(B_WIDTH, EVEN_BX), rows(B_WIDTH, EVEN_BGATE), rows(B_WIDTH, EVEN_CGATE),
            rows(A_WIDTH, 0), rows(M_WIDTH, 0),
            _layer_param((B_CONV, B_WIDTH), conv_layer), _layer_param((1, B_WIDTH), conv_layer),
            pl.BlockSpec((None, None, SUBLANES, B_WIDTH), lambda i, j: (halo_layer, i, 0, 0)),
            rows(D_MODEL, 0), _out_weight(layer), _layer_param((1, D_MODEL), layer), _layer_param((1, D_MODEL), layer),
        ],
        out_specs=(rows(D_MODEL, 0), rows(D_MODEL, 0), rows(D_MODEL, 0),
                   pl.BlockSpec((None, SUBLANES, B_WIDTH), lambda i, j: (i, 0, 0))),
        scratch_shapes=[pltpu.VMEM((SUBLANES, B_WIDTH), F32)],
        compiler_params=_params(("arbitrary", "arbitrary")),
        name="even_tail",
    )(proj, proj, proj, proj, ya, ym, conv_w, conv_b, halo, x, w_out, ln_g, ln_b)


def _odd_tail_body(gate_ref, oc_ref, hc_ref, yd_ref, ym_ref, nw_ref, x_ref, w_ref, g_ref, b_ref, o_ref, o16_ref):
    for rows in _sub_blocks(x_ref.shape[0]):
        sg = _silu(gate_ref[rows, :])
        yc = hc_ref[rows, :] * jax.nn.sigmoid(oc_ref[rows, :]) * sg[:, 0:C_WIDTH]
        z = yd_ref[rows, :] * sg[:, C_WIDTH:C_WIDTH + D_WIDTH]
        yd = z * lax.rsqrt(jnp.mean(z * z, axis=-1, keepdims=True) + EPS) * nw_ref[...]
        parts = (yc, yd, ym_ref[rows, :] * sg[:, C_WIDTH + D_WIDTH:MIX_WIDTH])
        _residual_ln(_mix_matmul(parts, w_ref), rows, x_ref, g_ref, b_ref, o_ref, o16_ref)


def odd_tail(proj, hc, yd, ym, x, norm_w, norm_layer, w_out, ln_g, ln_b, layer, *, tm):
    m = x.shape[0]
    rows = lambda width, col: pl.BlockSpec((tm, width), lambda i, j: (i, col))
    return pl.pallas_call(
        _odd_tail_body,
        out_shape=(jax.ShapeDtypeStruct((m, D_MODEL), F32), jax.ShapeDtypeStruct((m, D_MODEL), BF16)),
        grid=(m // tm, 1),
        in_specs=[
            rows(MIX_WIDTH, 0), rows(C_WIDTH, ODD_OC), rows(C_WIDTH, 0), rows(D_WIDTH, 0), rows(M_WIDTH, 0),
            _layer_param((1, D_WIDTH), norm_layer),
            rows(D_MODEL, 0), _out_weight(layer), _layer_param((1, D_MODEL), layer), _layer_param((1, D_MODEL), layer),
        ],
        out_specs=(rows(D_MODEL, 0), rows(D_MODEL, 0)),
        compiler_params=_params(("parallel", "arbitrary")),
        name="odd_tail",
    )(proj, proj, hc, yd, ym, norm_w, x, w_out, ln_g, ln_b)


def _tri_masks(t):
    ri = lax.broadcasted_iota(jnp.int32, (t, t), 0)
    ci = lax.broadcasted_iota(jnp.int32, (t, t), 1)
    return ri >= ci, ri <= ci


def _cumsum_cols(lower, x):
    return jnp.dot(lower.astype(F32), x, precision=lax.Precision.HIGHEST, preferred_element_type=F32)


def _cumsum_rows(upper, x):
    return jnp.dot(x, upper.astype(F32), precision=lax.Precision.HIGHEST, preferred_element_type=F32)


def _prefix_max_rows(x):
    row = lax.broadcasted_iota(jnp.int32, x.shape, 0)
    s = 1
    while s < x.shape[0]:
        x = jnp.where(row >= s, jnp.maximum(x, pltpu.roll(x, s, 0)), x)
        s *= 2
    return x


def _state_spec(shape, layer):
    return pl.BlockSpec((None, 1) + shape, lambda i, j: (layer, i) + (0,) * len(shape))


def _state_out_spec(shape):
    return pl.BlockSpec((1,) + shape, lambda i, j: (i,) + (0,) * len(shape))


def _mlstm_body(q_ref, k_ref, v_ref, gc_ref, gr_ref, c0_ref, n0_ref, m0_ref,
                h_ref, c_ref, n_ref, m_ref):
    @pl.when(pl.program_id(1) == 0)
    def _():
        c_ref[...] = c0_ref[...]
        n_ref[...] = n0_ref[...]
        m_ref[...] = m0_ref[...]

    t = q_ref.shape[0]
    scale = C_HEAD_DIM ** -0.5
    lower, upper = _tri_masks(t)
    gc = gc_ref[...]
    gr = gr_ref[...]
    b_c = pltpu.roll(_cumsum_cols(lower, _log_sigmoid(gc)), GATE_COLS - FG_OFF, 1)
    b_r = _cumsum_rows(upper, _log_sigmoid(gr))[FG_OFF:FG_OFF + C_HEADS, :]
    c_c = gc - b_c
    c_r = gr[IG_OFF:IG_OFF + C_HEADS, :] - b_r
    m8 = m_ref[0]
    eye = lax.broadcasted_iota(jnp.int32, m8.shape, 0) == lax.broadcasted_iota(jnp.int32, m8.shape, 1)
    ms_row = jnp.sum(jnp.where(eye, m8, 0.0), axis=0, keepdims=True)
    big_m = jnp.maximum(ms_row, _prefix_max_rows(c_c))
    gw_all = jnp.exp(ms_row - big_m)
    floor_all = jnp.exp(-(b_c + big_m))
    m_last = big_m[t - 1:t, :]
    wj_all = jnp.exp(c_c - m_last)
    decay_all = jnp.exp(ms_row - m_last)
    m_new_all = b_c[t - 1:t, :] + m_last
    for h in range(C_HEADS):
        hs = slice(h * C_HEAD_DIM, (h + 1) * C_HEAD_DIM)
        col = lambda a: a[:, h:h + 1]
        qf = q_ref[:, hs]
        kf = k_ref[:, hs] * scale
        vb = v_ref[:, hs].astype(BF16)
        q2, k2 = _split_bf16(qf), _split_bf16(kf)
        w = _split_dot(_nt_dot, q2, k2) * jnp.exp(jnp.where(lower, c_r[h:h + 1, :] - col(big_m), -jnp.inf))
        gw = col(gw_all)
        cs = c_ref[0, h]
        ns = n_ref[0, h:h + 1, :]
        w_hi, w_lo = _split_bf16(w)
        num = _dot(w_hi, vb) + _dot(w_lo, vb) + gw * _split_dot(_dot, q2, _split_bf16(cs))
        nq = jnp.sum(w, axis=1, keepdims=True) + gw * jnp.sum(qf * ns, axis=1, keepdims=True)
        h_ref[:, hs] = num * (1.0 / jnp.maximum(jnp.abs(nq), col(floor_all)))
        decay = col(decay_all)
        kw = kf * col(wj_all)
        kw_hi, kw_lo = _split_bf16(kw)
        c_ref[0, h] = decay * cs + _tn_dot(kw_hi, vb) + _tn_dot(kw_lo, vb)
        n_ref[0, h:h + 1, :] = decay * ns + jnp.sum(kw, axis=0, keepdims=True)
        m_ref[0, h:h + 1, :] = jnp.broadcast_to(col(m_new_all), (1, C_HEAD_DIM))


def mlstm(proj, gates_c, gates_r, c0, n0, m0, layer, *, batch, seq, t):
    nb = seq // t
    row = lambda i, j: i * nb + j
    rows = lambda width, col: pl.BlockSpec((t, width), lambda i, j: (row(i, j), col))
    shapes = (c0.shape[2:], n0.shape[2:], m0.shape[2:])
    return pl.pallas_call(
        _mlstm_body,
        out_shape=(jax.ShapeDtypeStruct((batch * seq, C_WIDTH), F32),)
        + tuple(jax.ShapeDtypeStruct((batch,) + s, F32) for s in shapes),
        grid=(batch, nb),
        in_specs=[
            rows(C_WIDTH, ODD_QC), rows(C_WIDTH, ODD_KC), rows(C_WIDTH, ODD_VC), rows(GATE_COLS, 0),
            pl.BlockSpec((None, N_GATES, t), lambda i, j: (row(i, j), 0, 0)),
        ] + [_state_spec(s, layer) for s in shapes],
        out_specs=(rows(C_WIDTH, 0),) + tuple(_state_out_spec(s) for s in shapes),
        compiler_params=_params(("arbitrary", "arbitrary")),
        name="mlstm",
    )(proj, proj, proj, gates_c, gates_r, c0, n0, m0)


def _ssd_body(xd_ref, bc_ref, gc_ref, gr_ref, wx_ref, wbc_ref, bx_ref, bbc_ref,
              alr_ref, alc_ref, dskip_ref, hx0_ref, hbc0_ref, s0_ref,
              y_ref, s_ref, hxo_ref, hbco_ref, hx, hbc):
    @pl.when(pl.program_id(1) == 0)
    def _():
        s_ref[...] = s0_ref[...]
        hx[...] = hx0_ref[0]
        hbc[...] = hbc0_ref[0]

    t = xd_ref.shape[0]
    xr = xd_ref[...]
    br = bc_ref[...]
    xs = _silu(_causal_dwconv(xr, hx[...], wx_ref, bx_ref))
    bcs = _silu(_causal_dwconv(br, hbc[...], wbc_ref, bbc_ref))
    hx[...] = xr[t - SUBLANES:t]
    hbc[...] = br[t - SUBLANES:t]
    hxo_ref[0] = xr[t - SUBLANES:t]
    hbco_ref[0] = br[t - SUBLANES:t]

    lower, upper = _tri_masks(t)
    dt_c = _softplus(gc_ref[:, DT_OFF:DT_OFF + D_HEADS])
    dt_r = _softplus(gr_ref[DT_OFF:DT_OFF + D_HEADS, :])
    a_c = _cumsum_cols(lower, dt_c * -jnp.exp(alr_ref[...]))
    a_r = _cumsum_rows(upper, dt_r * -jnp.exp(alc_ref[...]))
    for g in range(D_GROUPS):
        bmg = bcs[:, g * D_STATE:(g + 1) * D_STATE].astype(BF16)
        cmg = bcs[:, (D_GROUPS + g) * D_STATE:(D_GROUPS + g + 1) * D_STATE].astype(BF16)
        cb = _nt_dot(cmg, bmg)
        for hh in range(D_HG):
            h = g * D_HG + hh
            hs = slice(h * D_HEAD_DIM, (h + 1) * D_HEAD_DIM)
            acb = jnp.broadcast_to(a_c[:, h:h + 1], (t, max(t, D_HEAD_DIM)))
            ace = acb[:, :D_HEAD_DIM]
            lw = jnp.exp(jnp.where(lower, acb[:, :t] - a_r[h:h + 1, :], -jnp.inf))
            xh = xs[:, hs]
            sh = s_ref[0, h]
            y = _dot((cb * (lw * dt_r[h:h + 1, :])).astype(BF16), xh.astype(BF16))
            y = y + jnp.exp(ace) * _nt_dot(cmg, sh.astype(BF16))
            y_ref[:, hs] = y + dskip_ref[:, hs] * xh
            at = a_c[t - 1:t, h:h + 1]
            wj = jnp.exp(at - ace) * dt_c[:, h:h + 1]
            s_ref[0, h] = jnp.exp(at) * sh + _tn_dot((xh * wj).astype(BF16), bmg)


def ssd(proj, gates_c, gates_r, wx, wbc, bx, bbc, alog_r, alog_c, dskip, hx0, hbc0, s0, layer, state_layer,
        *, batch, seq, t):
    nb = seq // t
    row = lambda i, j: i * nb + j
    rows = lambda width, col: pl.BlockSpec((t, width), lambda i, j: (row(i, j), col))
    param = lambda a: _layer_param(a.shape[1:], layer)
    shapes = (s0.shape[2:], hx0.shape[2:], hbc0.shape[2:])
    return pl.pallas_call(
        _ssd_body,
        out_shape=(jax.ShapeDtypeStruct((batch * seq, D_WIDTH), F32),)
        + tuple(jax.ShapeDtypeStruct((batch,) + s, F32) for s in shapes),
        grid=(batch, nb),
        in_specs=[
            rows(D_WIDTH, ODD_XD), rows(D_BC, ODD_BC_BLOCK),
            rows(GATE_COLS, 0), pl.BlockSpec((None, N_GATES, t), lambda i, j: (row(i, j), 0, 0)),
            param(wx), param(wbc), param(bx), param(bbc), param(alog_r), param(alog_c), param(dskip),
            _state_spec(shapes[1], state_layer), _state_spec(shapes[2], state_layer), _state_spec(shapes[0], state_layer),
        ],
        out_specs=(rows(D_WIDTH, 0),) + tuple(_state_out_spec(s) for s in shapes),
        scratch_shapes=[pltpu.VMEM((SUBLANES, D_WIDTH), F32), pltpu.VMEM((SUBLANES, D_BC), F32)],
        compiler_params=_params(("arbitrary", "arbitrary")),
        name="ssd",
    )(proj, proj, gates_c, gates_r, wx, wbc, bx, bbc, alog_r, alog_c, dskip, hx0, hbc0, s0)


def _pad_state_rows(buf):
    pad = jnp.zeros(buf.shape[:-2] + (SUBLANES - buf.shape[-2], buf.shape[-1]), F32)
    return jnp.concatenate([pad, buf], axis=-2)


def _row_gates(gates_c, t):
    m = gates_c.shape[0]
    return gates_c[:, :N_GATES].reshape(m // t, t, N_GATES).transpose(0, 2, 1)


def _even_col_map(j):
    n_gate, n_head = MIX_WIDTH // 512, (3 * A_WIDTH + 3 * B_WIDTH) // 512
    return jnp.where(j < n_gate, j + n_head + 1, jnp.where(j == n_gate, n_head, j - n_gate - 1))


ODD_C4 = 4 * C_WIDTH
ODD_XBC0 = ODD_C4 + 2 * C_HEADS
ODD_DT0 = ODD_XBC0 + D_XBC
ODD_QM0 = ODD_DT0 + D_HEADS
ODD_GATE0 = ODD_QM0 + M_WIDTH
ODD_BIG = MIX_WIDTH + M_WIDTH + ODD_C4 + D_XBC
PREP_COLS = 512


def _odd_src_col(c):
    n_gate, n_c4 = MIX_WIDTH // PREP_COLS, ODD_C4 // PREP_COLS
    return jnp.where(c < n_gate, ODD_GATE0 + PREP_COLS * c,
                     jnp.where(c == n_gate, ODD_QM0,
                               jnp.where(c <= n_gate + n_c4, PREP_COLS * (c - n_gate - 1),
                                         ODD_XBC0 + PREP_COLS * (c - n_gate - 1 - n_c4))))


def _prep_big_body(wt_ref, o_ref):
    o_ref[...] = wt_ref[0].T.astype(BF16)


def _prep_lo_body(wt_ref, o_ref):
    o_ref[...] = _split_bf16(wt_ref[...].T)[1]


def _prep_gates_body(a_ref, b_ref, hi_ref, lo_ref):
    pad = jnp.zeros((GATE_COLS - N_GATES, a_ref.shape[1]), F32)
    hi, lo = _split_bf16(jnp.concatenate([a_ref[...], b_ref[...], pad], axis=0).T)
    hi_ref[...] = hi
    lo_ref[...] = lo


def _odd_weights(w_in, b_in):
    layers, k, n = w_in.shape
    wt = jnp.swapaxes(w_in, 1, 2)
    out_block = pl.BlockSpec((None, k, PREP_COLS), lambda l, c: (l, 0, c))
    big_w = pl.pallas_call(
        _prep_big_body,
        out_shape=jax.ShapeDtypeStruct((layers, k, ODD_BIG), BF16),
        grid=(layers, ODD_BIG // PREP_COLS),
        in_specs=[pl.BlockSpec((pl.Element(1), pl.Element(PREP_COLS), pl.Element(k)),
                               lambda l, c: (l, pl.multiple_of(_odd_src_col(c), 2 * C_HEADS), 0))],
        out_specs=out_block,
        compiler_params=_params(("parallel", "parallel")),
        name="odd_weight_prep",
    )(wt)
    qk_lo = pl.pallas_call(
        _prep_lo_body,
        out_shape=jax.ShapeDtypeStruct((layers, k, 2 * C_WIDTH), BF16),
        grid=(layers, 2 * C_WIDTH // PREP_COLS),
        in_specs=[pl.BlockSpec((None, PREP_COLS, k), lambda l, c: (l, c, 0))],
        out_specs=out_block,
        compiler_params=_params(("parallel", "parallel")),
        name="odd_weight_prep_lo",
    )(wt)
    rows16 = lambda first: pl.BlockSpec((None, 2 * C_HEADS, k), lambda l: (l, first // (2 * C_HEADS), 0))
    gate_out = pl.BlockSpec((None, k, GATE_COLS), lambda l: (l, 0, 0))
    assert D_HEADS == 2 * C_HEADS and ODD_C4 % D_HEADS == 0 and ODD_DT0 % D_HEADS == 0
    small_hi, small_lo = pl.pallas_call(
        _prep_gates_body,
        out_shape=(jax.ShapeDtypeStruct((layers, k, GATE_COLS), BF16),) * 2,
        grid=(layers,),
        in_specs=[rows16(ODD_C4), rows16(ODD_DT0)],
        out_specs=(gate_out, gate_out),
        compiler_params=_params(("parallel",)),
        name="odd_weight_prep_gates",
    )(wt, wt)
    big = lambda a: jnp.concatenate(
        [a[..., ODD_GATE0:], a[..., ODD_QM0:ODD_GATE0], a[..., :ODD_C4], a[..., ODD_XBC0:ODD_DT0]], axis=-1)
    small = lambda a: jnp.concatenate(
        [a[..., ODD_C4:ODD_XBC0], a[..., ODD_DT0:ODD_QM0], jnp.zeros(a.shape[:-1] + (GATE_COLS - N_GATES,), a.dtype)],
        axis=-1)
    return (big_w, big(b_in)[:, None, :], qk_lo), (small_hi, small(b_in)[:, None, :], small_lo)


def _heads(a, batch, seq, col, last):
    part = a.reshape(batch, seq, a.shape[1])[:, seq - last:, col * A_WIDTH:(col + 1) * A_WIDTH]
    return part.reshape(batch, last, A_HEADS, A_HEAD_DIM)


def kernel(x_prompt, x_sample, mem_prompt, cache_a_k, cache_a_v, cache_mem_k, cache_mem_v, state_b_conv, state_c_C, state_c_n, state_c_m, state_d_ssm, state_d_conv, w_in_even, b_in_even, w_in_odd, b_in_odd, w_out, w_mem_kv, ln_g, ln_b, rel_bias_table, b_conv_w, b_conv_b, d_conv_w, d_conv_b, d_A_log, d_skip, d_norm_w):
    bp, sp, _ = x_prompt.shape
    bs, ss, _ = x_sample.shape
    lc = cache_a_k.shape[2]
    xp = x_prompt.reshape(bp * sp, D_MODEL)
    xs = x_sample.reshape(bs * ss, D_MODEL)
    xp16, xs16 = xp.astype(BF16), xs.astype(BF16)
    mem = mem_prompt.reshape(bp * MEM_LEN, D_MODEL).astype(BF16)

    we, be = w_in_even.astype(BF16), b_in_even[:, None, :]
    (wo_big, bo_big, wo_qk_lo), (wo_small, bo_small, wo_small_lo) = _odd_weights(w_in_odd, b_in_odd)
    qk_blocks = (ODD_QC * C_WIDTH // 512, (ODD_KC + 1) * C_WIDTH // 512)
    w_out_b = w_out.astype(BF16)
    w_mem_b = w_mem_kv.astype(BF16)
    zero_bias = jnp.zeros((1, 1, 2 * M_WIDTH), F32)
    ln_g3, ln_b3 = ln_g[:, None, :], ln_b[:, None, :]
    conv_b3 = b_conv_b[:, None, :]
    norm_w3 = d_norm_w[:, None, :]
    d_wx, d_wbc = d_conv_w[:, :, :D_WIDTH], d_conv_w[:, :, D_WIDTH:]
    d_bx, d_bbc = d_conv_b[:, None, :D_WIDTH], d_conv_b[:, None, D_WIDTH:]
    alog_r, alog_c = d_A_log[:, None, :], d_A_log[:, :, None]
    dskip = jnp.repeat(d_skip, D_HEAD_DIM, axis=-1)[:, None, :]

    s_halo_b = _pad_state_rows(state_b_conv)
    s_halo_d = _pad_state_rows(state_d_conv)
    s_hx0, s_hbc0 = s_halo_d[..., :D_WIDTH], s_halo_d[..., D_WIDTH:]
    s_m0 = jnp.broadcast_to(state_c_m[..., None], state_c_m.shape + (C_HEAD_DIM,))
    zeros = lambda *shape: jnp.zeros((1, bp) + shape, F32)
    p_halo_b, p_hx0, p_hbc0 = zeros(SUBLANES, B_WIDTH), zeros(SUBLANES, D_WIDTH), zeros(SUBLANES, D_BC)
    p_c0, p_n0, p_m0 = zeros(C_HEADS, C_HEAD_DIM, C_HEAD_DIM), zeros(C_HEADS, C_HEAD_DIM), zeros(C_HEADS, C_HEAD_DIM)
    p_s0 = zeros(D_HEADS, D_HEAD_DIM, D_STATE)

    outs = {k: [] for k in ("p_mk", "p_mv", "p_ak", "p_av", "p_bc", "p_cc", "p_cn", "p_cm", "p_ds", "p_dc",
                            "s_ak", "s_av", "s_bc", "s_cc", "s_cn", "s_cm", "s_ds", "s_dc")}
    tp = min(SCAN_BLOCK, sp)
    ts = min(SCAN_BLOCK, ss)
    for l in range(DEPTH):
        i = l // 2
        mkv = linear(mem, w_mem_b, zero_bias.repeat(DEPTH, 0), l)
        mkv5 = mkv.reshape(1, bp, MEM_LEN, 2 * M_HEADS, M_HEAD_DIM)
        mk5, mv5 = mkv5[:, :, :, :M_HEADS], mkv5[:, :, :, M_HEADS:]
        outs["p_mk"].append(mk5[0])
        outs["p_mv"].append(mv5[0])
        flat = lambda a: a.reshape(a.shape[0], a.shape[1], MEM_LEN * M_HEADS, M_HEAD_DIM)
        p_mem = (flat(mk5), flat(mv5), lambda b: (0, b, 0, 0), lambda b: (0, b, 0, 0))
        s_mem = (flat(cache_mem_k), flat(cache_mem_v), lambda b, l=l: (l, b, 0, 0), lambda b, l=l: (l, b, 0, 0))
        if l % 2 == 0:
            proj_p = linear(xp16, we, be, i, col_map=_even_col_map)
            proj_s = linear(xs16, we, be, i, col_map=_even_col_map)
            ya_p = band_attention_prompt(proj_p, rel_bias_band(rel_bias_table[i]), tq=A_WINDOW)
            bias_s = rel_bias(rel_bias_table[i], ss, lc + ss, lc)
            rows_of = lambda a: a.reshape(a.shape[0], a.shape[1], lc * A_HEADS, A_HEAD_DIM)
            ya_s = band_attention_sample(proj_s, rows_of(cache_a_k), rows_of(cache_a_v), i,
                                         bias_s[:, :, :lc], bias_s[:, :, lc:], t=ss)
            ym_p = mem_attention(proj_p, *p_mem, batch=bp, seq=sp, tl=min(sp, 512))
            ym_s = mem_attention(proj_s, *s_mem, batch=bs, seq=ss, tl=min(ss, 512))
            tail = (b_conv_w, conv_b3)
            xp_new, xp16_new, xplo_new, tail_p = even_tail(proj_p, ya_p, ym_p, xp, *tail, p_halo_b, 0, w_out_b, ln_g3, ln_b3,
                                                 i, l, seq=sp, tm=min(sp, 256))
            xs_new, xs16_new, xslo_new, tail_s = even_tail(proj_s, ya_s, ym_s, xs, *tail, s_halo_b, i, w_out_b, ln_g3, ln_b3,
                                                 i, l, seq=ss, tm=min(ss, 256))
            outs["p_ak"].append(_heads(proj_p, bp, sp, EVEN_KA, A_WINDOW))
            outs["p_av"].append(_heads(proj_p, bp, sp, EVEN_VA, A_WINDOW))
            outs["s_ak"].append(_heads(proj_s, bs, ss, EVEN_KA, ss))
            outs["s_av"].append(_heads(proj_s, bs, ss, EVEN_VA, ss))
            outs["p_bc"].append(tail_p[:, SUBLANES - (B_CONV - 1):])
            outs["s_bc"].append(tail_s[:, SUBLANES - (B_CONV - 1):])
        else:
            ssd_params = (d_wx, d_wbc, d_bx, d_bbc, alog_r, alog_c, dskip)
            res = []
            for x, x16, xlo, batch, seq, t, mem_args, states, sl in (
                    (xp, xp16, xplo, bp, sp, tp, p_mem, (p_c0, p_n0, p_m0, p_hx0, p_hbc0, p_s0), 0),
                    (xs, xs16, xslo, bs, ss, ts, s_mem, (state_c_C, state_c_n, s_m0, s_hx0, s_hbc0, state_d_ssm), i)):
                c0, n0, m0, hx0, hbc0, s0 = states
                proj = linear(x16, wo_big, bo_big, i,
                              fine=(xlo, wo_qk_lo, *qk_blocks) if l + 2 < DEPTH else None)
                gates_c = linear(x16, wo_small, bo_small, i, tn=GATE_COLS, fine=(xlo, wo_small_lo, 0, 1))
                gates_r = _row_gates(gates_c, t)
                hc, cc, cn, cm = mlstm(proj, gates_c, gates_r, c0, n0, m0, sl, batch=batch, seq=seq, t=t)
                yd, s_new, hx, hbc = ssd(proj, gates_c, gates_r, *ssd_params, hx0, hbc0, s0, i, sl,
                                         batch=batch, seq=seq, t=t)
                ym = mem_attention(proj, *mem_args, batch=batch, seq=seq, tl=min(seq, 512))
                x_new, x16_new = odd_tail(proj, hc, yd, ym, x, norm_w3, i, w_out_b, ln_g3, ln_b3, l, tm=256)
                dbuf = jnp.concatenate([hx, hbc], axis=-1)[:, SUBLANES - (D_CONV - 1):]
                res.append((x_new, x16_new, cc, cn, cm[:, :, 0], s_new, dbuf))
            (xp_new, xp16_new, *st_p), (xs_new, xs16_new, *st_s) = res
            for key, val in zip(("p_cc", "p_cn", "p_cm", "p_ds", "p_dc"), st_p):
                outs[key].append(val)
            for key, val in zip(("s_cc", "s_cn", "s_cm", "s_ds", "s_dc"), st_s):
                outs[key].append(val)
        xp, xs, xp16, xs16 = xp_new, xs_new, xp16_new, xs16_new
        if l % 2 == 0:
            xplo, xslo = xplo_new, xslo_new
    st = jnp.stack
    return (xp.reshape(bp, sp, D_MODEL), xs.reshape(bs, ss, D_MODEL),
            st(outs["p_mk"]), st(outs["p_mv"]), st(outs["p_ak"]), st(outs["p_av"]), st(outs["p_bc"]),
            st(outs["p_cc"]), st(outs["p_cn"]), st(outs["p_cm"]), st(outs["p_ds"]), st(outs["p_dc"]),
            st(outs["s_ak"]), st(outs["s_av"]), st(outs["s_bc"]),
            st(outs["s_cc"]), st(outs["s_cn"]), st(outs["s_cm"]), st(outs["s_ds"]), st(outs["s_dc"]))
```

```python
import functools

import jax
import jax.numpy as jnp
from jax import lax
from jax.experimental import pallas as pl
from jax.experimental.pallas import tpu as pltpu

F32 = jnp.float32
BF16 = jnp.bfloat16

D_MODEL = 2048
DEPTH = 4
CHUNK = 64
ALPHA = (2 * DEPTH) ** 0.25
EPS = 1e-5
NEG = -1e30

A_WIDTH = D_MODEL // 2
A_HEAD_DIM = 128
A_HEADS = A_WIDTH // A_HEAD_DIM
A_BAND = 8
A_WINDOW = A_BAND * CHUNK
REL_CLIP = 128
B_WIDTH = D_MODEL // 2
B_CONV = 3
C_WIDTH = D_MODEL // 2
C_HEAD_DIM = 128
C_HEADS = C_WIDTH // C_HEAD_DIM
D_WIDTH = D_MODEL // 2
D_HEAD_DIM = 64
D_HEADS = D_WIDTH // D_HEAD_DIM
D_GROUPS = 2
D_HG = D_HEADS // D_GROUPS
D_STATE = 128
D_CONV = 4
D_BC = 2 * D_GROUPS * D_STATE
D_XBC = D_WIDTH + D_BC
MEM_LEN = 256
M_WIDTH = D_MODEL // 4
M_HEADS = 4
M_HEAD_DIM = M_WIDTH // M_HEADS
MIX_WIDTH = A_WIDTH + B_WIDTH + M_WIDTH

LANES = 128
SUBLANES = 8
VMEM_LIMIT_BYTES = 56 * 1024 * 1024

EVEN_QA, EVEN_KA, EVEN_VA, EVEN_BX, EVEN_BGATE, EVEN_CGATE = 3, 4, 5, 6, 7, 8
ODD_QC, ODD_KC, ODD_VC, ODD_OC, ODD_XD = 3, 4, 5, 6, 7
QM_BLOCK = MIX_WIDTH // M_WIDTH
ODD_BC_BLOCK = (MIX_WIDTH + M_WIDTH + 5 * 1024) // D_BC
GATE_COLS = LANES
IG_OFF, FG_OFF, DT_OFF = 0, C_HEADS, 2 * C_HEADS
N_GATES = 2 * C_HEADS + D_HEADS

SCAN_BLOCK = 256
Q_SUB = 256
K_SUB = Q_SUB + A_WINDOW
BAND_ROWS = (A_BAND + 1) * CHUNK
LOG2E = 1.4426950408889634
LINEAR_ROWS = 2048
TAIL_SUB = 128
MEM_GROUP = 4
TAIL_GROUP = 8


def _params(semantics):
    return pltpu.CompilerParams(dimension_semantics=semantics, vmem_limit_bytes=VMEM_LIMIT_BYTES)


def _nt_dot(a, b):
    return lax.dot_general(a, b, (((1,), (1,)), ((), ())), preferred_element_type=F32)


def _tn_dot(a, b):
    return lax.dot_general(a, b, (((0,), (0,)), ((), ())), preferred_element_type=F32)


def _dot(a, b):
    return jnp.dot(a, b, preferred_element_type=F32)


def _split_dot(dot, a, b):
    return dot(a[0], b[0]) + (dot(a[0], b[1]) + dot(a[1], b[0]))


def _silu(x):
    return x * jax.nn.sigmoid(x)


def _softplus(x):
    return jnp.maximum(x, 0.0) + jnp.log1p(jnp.exp(-jnp.abs(x)))


def _log_sigmoid(x):
    return -_softplus(-x)


def _softmax_pv(s, v, exp=jnp.exp):
    m = jnp.max(s, axis=-1, keepdims=True)
    p = exp(s - m)
    inv = 1.0 / jnp.sum(p, axis=-1, keepdims=True)
    return _dot(p.astype(BF16), v) * inv


def _split_bf16(a):
    hi = a.astype(BF16)
    return hi, (a - hi.astype(F32)).astype(BF16)


def _linear_body(x_ref, w_ref, b_ref, o_ref):
    o_ref[...] = _dot(x_ref[...], w_ref[...]) + b_ref[...]


def _linear_fine_body(x_ref, w_ref, b_ref, xl_ref, wl_ref, o_ref, *, fine):
    j = pl.program_id(1)
    o_ref[...] = _dot(x_ref[...], w_ref[...]) + b_ref[...]

    @pl.when((j >= fine[0]) & (j < fine[1]))
    def _():
        o_ref[...] += _dot(xl_ref[...], w_ref[...]) + _dot(x_ref[...], wl_ref[...])


def linear(x, w, b, layer, *, tn=512, fine=None, col_map=lambda j: j):
    m, k = x.shape
    n = w.shape[2]
    tm = min(m, LINEAR_ROWS)
    assert m % tm == 0 and n % tn == 0, (m, n, tm, tn)
    in_specs = [
        pl.BlockSpec((tm, k), lambda i, j: (i, 0)),
        pl.BlockSpec((None, k, tn), lambda i, j: (layer, 0, col_map(j))),
        pl.BlockSpec((None, 1, tn), lambda i, j: (layer, 0, col_map(j))),
    ]
    body, args = _linear_body, (x, w, b)
    if fine is not None:
        x_lo, w_lo, j0, j1 = fine
        assert w_lo.shape[2] == (j1 - j0) * tn
        in_specs += [
            pl.BlockSpec((tm, k), lambda i, j: (i, 0)),
            pl.BlockSpec((None, k, tn), lambda i, j: (layer, 0, jnp.clip(j - j0, 0, j1 - j0 - 1))),
        ]
        body, args = functools.partial(_linear_fine_body, fine=(j0, j1)), (x, w, b, x_lo, w_lo)
    return pl.pallas_call(
        body,
        out_shape=jax.ShapeDtypeStruct((m, n), F32),
        grid=(m // tm, n // tn),
        in_specs=in_specs,
        out_specs=pl.BlockSpec((tm, tn), lambda i, j: (i, j)),
        compiler_params=_params(("parallel", "arbitrary")),
        name="linear",
    )(*args)


def _table_select(tab_ref, h, idx):
    def step(t, acc):
        return jnp.where(idx == t, tab_ref[h, t], acc)

    return lax.fori_loop(0, 2 * REL_CLIP + 1, step, jnp.zeros(idx.shape, F32))


def _relbias_body(tab_ref, o_ref, *, offset):
    ql, kl = o_ref.shape[1], o_ref.shape[2]
    qi = lax.broadcasted_iota(jnp.int32, (ql, kl), 0)
    ki = lax.broadcasted_iota(jnp.int32, (ql, kl), 1)
    idx = jnp.clip(qi + offset - ki, -REL_CLIP, REL_CLIP) + REL_CLIP
    o_ref[0] = _table_select(tab_ref, pl.program_id(0), idx)


def rel_bias(table, q_len, k_len, offset):
    heads = table.shape[0]
    return pl.pallas_call(
        functools.partial(_relbias_body, offset=offset),
        out_shape=jax.ShapeDtypeStruct((heads, q_len, k_len), F32),
        grid=(heads,),
        in_specs=[pl.BlockSpec(memory_space=pltpu.SMEM)],
        out_specs=pl.BlockSpec((1, q_len, k_len), lambda h: (h, 0, 0)),
        compiler_params=_params(("arbitrary",)),
        name="rel_bias",
    )(table)


def _relbias_band_body(tab_ref, o_ref):
    period = 2 * A_WINDOW
    u = lax.broadcasted_iota(jnp.int32, (SUBLANES, period), 1)
    d = jnp.where(u < K_SUB, u, u - period)
    idx = jnp.clip(A_WINDOW - d, -REL_CLIP, REL_CLIP) + REL_CLIP
    g = _table_select(tab_ref, pl.program_id(0), idx)
    full = pltpu.roll(jnp.broadcast_to(g[0:1], (Q_SUB, period)), 0, 1, stride=1, stride_axis=0)
    r = lax.broadcasted_iota(jnp.int32, (Q_SUB, K_SUB), 0)
    w = lax.broadcasted_iota(jnp.int32, (Q_SUB, K_SUB), 1)
    first = (r // CHUNK) * CHUNK
    in_band = (w >= first) & (w < first + BAND_ROWS)
    o_ref[0] = jnp.where(in_band, full[:, :K_SUB] * LOG2E, NEG)


def rel_bias_band(table):
    heads = table.shape[0]
    return pl.pallas_call(
        _relbias_band_body,
        out_shape=jax.ShapeDtypeStruct((heads, Q_SUB, K_SUB), F32),
        grid=(heads,),
        in_specs=[pl.BlockSpec(memory_space=pltpu.SMEM)],
        out_specs=pl.BlockSpec((1, Q_SUB, K_SUB), lambda h: (h, 0, 0)),
        compiler_params=_params(("arbitrary",)),
        name="rel_bias_band",
    )(table)


def _band_prompt_body(q_ref, k_ref, v_ref, bias_ref, o_ref, kwin, vwin):
    i = pl.program_id(0)
    tq = q_ref.shape[0]
    scale = A_HEAD_DIM ** -0.5 * LOG2E

    @pl.when(i == 0)
    def _():
        kwin[0:tq] = jnp.zeros((tq, A_WIDTH), BF16)
        vwin[0:tq] = jnp.zeros((tq, A_WIDTH), BF16)

    @pl.when(i > 0)
    def _():
        kwin[0:tq] = kwin[tq:2 * tq]
        vwin[0:tq] = vwin[tq:2 * tq]

    kwin[tq:2 * tq] = k_ref[...].astype(BF16)
    vwin[tq:2 * tq] = v_ref[...].astype(BF16)

    col = lax.broadcasted_iota(jnp.int32, (1, K_SUB), 1)
    unreal = jnp.where(i == 0, tq, 0)
    for r0 in range(0, tq, Q_SUB):
        neg = jnp.where(col + r0 < unreal, NEG, 0.0)
        for h in range(A_HEADS):
            hs = slice(h * A_HEAD_DIM, (h + 1) * A_HEAD_DIM)
            qh = (q_ref[r0:r0 + Q_SUB, hs] * scale).astype(BF16)
            s = _nt_dot(qh, kwin[r0:r0 + K_SUB, hs]) + bias_ref[h] + neg
            o_ref[r0:r0 + Q_SUB, hs] = _softmax_pv(s, vwin[r0:r0 + K_SUB, hs], jnp.exp2)


def band_attention_prompt(proj, bias, *, tq):
    s = proj.shape[0]
    assert tq >= A_WINDOW and tq % Q_SUB == 0 and Q_SUB % CHUNK == 0 and s % tq == 0
    return pl.pallas_call(
        _band_prompt_body,
        out_shape=jax.ShapeDtypeStruct((s, A_WIDTH), F32),
        grid=(s // tq,),
        in_specs=[
            pl.BlockSpec((tq, A_WIDTH), lambda i: (i, EVEN_QA)),
            pl.BlockSpec((tq, A_WIDTH), lambda i: (i, EVEN_KA)),
            pl.BlockSpec((tq, A_WIDTH), lambda i: (i, EVEN_VA)),
            pl.BlockSpec((A_HEADS, Q_SUB, K_SUB), lambda i: (0, 0, 0)),
        ],
        out_specs=pl.BlockSpec((tq, A_WIDTH), lambda i: (i, 0)),
        scratch_shapes=[pltpu.VMEM((2 * tq, A_WIDTH), BF16), pltpu.VMEM((2 * tq, A_WIDTH), BF16)],
        compiler_params=_params(("arbitrary",)),
        name="band_attention_prompt",
    )(proj, proj, proj, bias)


def _head_rows(ref, h, heads):
    return ref[pl.ds(h, ref.shape[0] // heads, stride=heads), :]


def _band_sample_body(q_ref, nk_ref, nv_ref, ck_ref, cv_ref, bc_ref, bn_ref, o_ref):
    scale = A_HEAD_DIM ** -0.5
    for h in range(A_HEADS):
        hs = slice(h * A_HEAD_DIM, (h + 1) * A_HEAD_DIM)
        qh = q_ref[:, hs].astype(BF16)
        s1 = _nt_dot(qh, _head_rows(ck_ref, h, A_HEADS).astype(BF16)) * scale + bc_ref[h]
        s2 = _nt_dot(qh, nk_ref[:, hs].astype(BF16)) * scale + bn_ref[h]
        m = jnp.maximum(jnp.max(s1, axis=-1, keepdims=True), jnp.max(s2, axis=-1, keepdims=True))
        p1 = jnp.exp(s1 - m)
        p2 = jnp.exp(s2 - m)
        inv = 1.0 / (jnp.sum(p1, axis=-1, keepdims=True) + jnp.sum(p2, axis=-1, keepdims=True))
        o = _dot(p1.astype(BF16), _head_rows(cv_ref, h, A_HEADS).astype(BF16))
        o = o + _dot(p2.astype(BF16), nv_ref[:, hs].astype(BF16))
        o_ref[:, hs] = o * inv


def band_attention_sample(proj, cache_k, cache_v, layer, bias_cache, bias_new, *, t):
    _, b, rows, _ = cache_k.shape
    lc = rows // A_HEADS
    cache = pl.BlockSpec((None, None, rows, A_HEAD_DIM), lambda i: (layer, i, 0, 0))
    return pl.pallas_call(
        _band_sample_body,
        out_shape=jax.ShapeDtypeStruct((b * t, A_WIDTH), F32),
        grid=(b,),
        in_specs=[
            pl.BlockSpec((t, A_WIDTH), lambda i: (i, EVEN_QA)),
            pl.BlockSpec((t, A_WIDTH), lambda i: (i, EVEN_KA)),
            pl.BlockSpec((t, A_WIDTH), lambda i: (i, EVEN_VA)),
            cache, cache,
            pl.BlockSpec((A_HEADS, t, lc), lambda i: (0, 0, 0)),
            pl.BlockSpec((A_HEADS, t, t), lambda i: (0, 0, 0)),
        ],
        out_specs=pl.BlockSpec((t, A_WIDTH), lambda i: (i, 0)),
        compiler_params=_params(("parallel",)),
        name="band_attention_sample",
    )(proj, proj, proj, cache_k, cache_v, bias_cache, bias_new)


def _mem_attn_body(q_ref, mk_ref, mv_ref, o_ref):
    scale = M_HEAD_DIM ** -0.5
    group = mk_ref.shape[0]
    tl = q_ref.shape[0] // group
    for g in range(group):
        rows = slice(g * tl, (g + 1) * tl)
        for h in range(M_HEADS):
            hs = slice(h * M_HEAD_DIM, (h + 1) * M_HEAD_DIM)
            s = _nt_dot(q_ref[rows, hs].astype(BF16), _head_rows(mk_ref.at[g], h, M_HEADS).astype(BF16)) * scale
            o_ref[rows, hs] = _softmax_pv(s, _head_rows(mv_ref.at[g], h, M_HEADS).astype(BF16))


def mem_attention(proj, mk, mv, kmap, vmap, *, batch, seq, tl):
    nt = seq // tl
    group = MEM_GROUP if (nt == 1 and batch % MEM_GROUP == 0) else 1
    mem = lambda f: pl.BlockSpec((None, group, MEM_LEN * M_HEADS, M_HEAD_DIM), lambda i, j: f(i))
    return pl.pallas_call(
        _mem_attn_body,
        out_shape=jax.ShapeDtypeStruct((batch * seq, M_WIDTH), F32),
        grid=(batch // group, nt),
        in_specs=[pl.BlockSpec((group * tl, M_WIDTH), lambda i, j: (i * nt + j, QM_BLOCK)), mem(kmap), mem(vmap)],
        out_specs=pl.BlockSpec((group * tl, M_WIDTH), lambda i, j: (i * nt + j, 0)),
        compiler_params=_params(("parallel", "parallel")),
        name="mem_attention",
    )(proj, mk, mv)


def _shift_rows(x, halo, s):
    rolled = pltpu.roll(x, s, 0)
    halo_rolled = pltpu.roll(halo, s, 0)
    row = lax.broadcasted_iota(jnp.int32, halo.shape, 0)
    top = jnp.where(row < s, halo_rolled, rolled[0:SUBLANES])
    if x.shape[0] == SUBLANES:
        return top
    return jnp.concatenate([top, rolled[SUBLANES:]], axis=0)


def _causal_dwconv(x, halo, w_ref, b_ref):
    width = w_ref.shape[0]
    y = b_ref[...]
    for j in range(width):
        s = width - 1 - j
        xs = x if s == 0 else _shift_rows(x, halo, s)
        y = y + xs * w_ref[j:j + 1, :]
    return y


def _residual_ln(acc, rows, x_ref, g_ref, b_ref, o_ref, o16_ref, olo_ref=None):
    y = ALPHA * x_ref[rows, :] + acc
    mu = jnp.mean(y, axis=-1, keepdims=True)
    d = y - mu
    var = jnp.mean(d * d, axis=-1, keepdims=True)
    out = d * lax.rsqrt(var + EPS) * g_ref[...] + b_ref[...]
    o_ref[rows, :] = out
    hi, lo = _split_bf16(out)
    o16_ref[rows, :] = hi
    if olo_ref is not None:
        olo_ref[rows, :] = lo


def _sub_blocks(tm):
    sub = min(TAIL_SUB, tm)
    return [slice(r, r + sub) for r in range(0, tm, sub)]


def _mix_matmul(parts, w_ref):
    acc, r0 = None, 0
    for p in parts:
        term = _dot(p.astype(BF16), w_ref[r0:r0 + p.shape[1], :])
        acc = term if acc is None else acc + term
        r0 += p.shape[1]
    return acc


def _even_tail_body(gate_ref, bx_ref, bg_ref, cg_ref, ya_ref, ym_ref, cw_ref, cb_ref, halo_ref,
                    x_ref, w_ref, g_ref, b_ref, o_ref, o16_ref, olo_ref, tail_ref, carry):
    @pl.when(pl.program_id(1) == 0)
    def _():
        carry[...] = halo_ref[...]

    tm = bx_ref.shape[0]
    seg = tm // carry.shape[0]
    conv_in = lambda rows: cg_ref[rows, :] * bx_ref[rows, :]
    for rows in _sub_blocks(tm):
        step = min(seg, rows.stop - rows.start)
        pieces = []
        for p0 in range(rows.start, rows.stop, step):
            part = slice(p0, p0 + step)
            halo = carry[p0 // seg] if p0 % seg == 0 else conv_in(slice(p0 - SUBLANES, p0))
            pieces.append(bg_ref[part, :] * _causal_dwconv(conv_in(part), halo, cw_ref, cb_ref))
        yb = pieces[0] if len(pieces) == 1 else jnp.concatenate(pieces, axis=0)
        sg = _silu(gate_ref[rows, :])
        parts = (ya_ref[rows, :] * sg[:, 0:A_WIDTH],
                 yb * sg[:, A_WIDTH:A_WIDTH + B_WIDTH],
                 ym_ref[rows, :] * sg[:, A_WIDTH + B_WIDTH:MIX_WIDTH])
        _residual_ln(_mix_matmul(parts, w_ref), rows, x_ref, g_ref, b_ref, o_ref, o16_ref, olo_ref)
    for g in range(carry.shape[0]):
        last = conv_in(slice((g + 1) * seg - SUBLANES, (g + 1) * seg))
        carry[g] = last
        tail_ref[g] = last


def _layer_param(shape, layer):
    return pl.BlockSpec((None,) + shape, lambda i, j: (layer,) + (0,) * len(shape))


def _out_weight(layer):
    return pl.BlockSpec((None, MIX_WIDTH, D_MODEL), lambda i, j: (layer, 0, 0), pipeline_mode=pl.Buffered(1))


def even_tail(proj, ya, ym, x, conv_w, conv_b, halo, halo_layer, w_out, ln_g, ln_b, conv_layer, layer, *, seq, tm):
    b = halo.shape[1]
    nt = seq // tm
    group = TAIL_GROUP if (nt == 1 and b % TAIL_GROUP == 0) else 1
    row = lambda i, j: i * nt + j
    rows = lambda width, col: pl.BlockSpec((group * tm, width), lambda i, j: (row(i, j), col))
    return pl.pallas_call(
        _even_tail_body,
        out_shape=(jax.ShapeDtypeStruct((b * seq, D_MODEL), F32),
                   jax.ShapeDtypeStruct((b * seq, D_MODEL), BF16),
                   jax.ShapeDtypeStruct((b * seq, D_MODEL), BF16),
                   jax.ShapeDtypeStruct((b, SUBLANES, B_WIDTH), F32)),
        grid=(b // group, nt),
        in_specs=[
            rows(MIX_WIDTH, 0), rows(B_WIDTH, EVEN_BX), rows(B_WIDTH, EVEN_BGATE), rows(B_WIDTH, EVEN_CGATE),
            rows(A_WIDTH, 0), rows(M_WIDTH, 0),
            _layer_param((B_CONV, B_WIDTH), conv_layer), _layer_param((1, B_WIDTH), conv_layer),
            pl.BlockSpec((None, group, SUBLANES, B_WIDTH), lambda i, j: (halo_layer, i, 0, 0)),
            rows(D_MODEL, 0), _out_weight(layer), _layer_param((1, D_MODEL), layer), _layer_param((1, D_MODEL), layer),
        ],
        out_specs=(rows(D_MODEL, 0), rows(D_MODEL, 0), rows(D_MODEL, 0),
                   pl.BlockSpec((group, SUBLANES, B_WIDTH), lambda i, j: (i, 0, 0))),
        scratch_shapes=[pltpu.VMEM((group, SUBLANES, B_WIDTH), F32)],
        compiler_params=_params(("arbitrary", "arbitrary")),
        name="even_tail",
    )(proj, proj, proj, proj, ya, ym, conv_w, conv_b, halo, x, w_out, ln_g, ln_b)


def _odd_tail_body(gate_ref, oc_ref, hc_ref, yd_ref, ym_ref, nw_ref, x_ref, w_ref, g_ref, b_ref, o_ref, o16_ref):
    for rows in _sub_blocks(x_ref.shape[0]):
        sg = _silu(gate_ref[rows, :])
        yc = hc_ref[rows, :] * jax.nn.sigmoid(oc_ref[rows, :]) * sg[:, 0:C_WIDTH]
        z = yd_ref[rows, :] * sg[:, C_WIDTH:C_WIDTH + D_WIDTH]
        yd = z * lax.rsqrt(jnp.mean(z * z, axis=-1, keepdims=True) + EPS) * nw_ref[...]
        parts = (yc, yd, ym_ref[rows, :] * sg[:, C_WIDTH + D_WIDTH:MIX_WIDTH])
        _residual_ln(_mix_matmul(parts, w_ref), rows, x_ref, g_ref, b_ref, o_ref, o16_ref)


def odd_tail(proj, hc, yd, ym, x, norm_w, norm_layer, w_out, ln_g, ln_b, layer, *, tm):
    m = x.shape[0]
    rows = lambda width, col: pl.BlockSpec((tm, width), lambda i, j: (i, col))
    return pl.pallas_call(
        _odd_tail_body,
        out_shape=(jax.ShapeDtypeStruct((m, D_MODEL), F32), jax.ShapeDtypeStruct((m, D_MODEL), BF16)),
        grid=(m // tm, 1),
        in_specs=[
            rows(MIX_WIDTH, 0), rows(C_WIDTH, ODD_OC), rows(C_WIDTH, 0), rows(D_WIDTH, 0), rows(M_WIDTH, 0),
            _layer_param((1, D_WIDTH), norm_layer),
            rows(D_MODEL, 0), _out_weight(layer), _layer_param((1, D_MODEL), layer), _layer_param((1, D_MODEL), layer),
        ],
        out_specs=(rows(D_MODEL, 0), rows(D_MODEL, 0)),
        compiler_params=_params(("parallel", "arbitrary")),
        name="odd_tail",
    )(proj, proj, hc, yd, ym, norm_w, x, w_out, ln_g, ln_b)


def _tri_masks(t):
    ri = lax.broadcasted_iota(jnp.int32, (t, t), 0)
    ci = lax.broadcasted_iota(jnp.int32, (t, t), 1)
    return ri >= ci, ri <= ci


def _cumsum_cols(lower, x):
    return jnp.dot(lower.astype(F32), x, precision=lax.Precision.HIGHEST, preferred_element_type=F32)


def _cumsum_rows(upper, x):
    return jnp.dot(x, upper.astype(F32), precision=lax.Precision.HIGHEST, preferred_element_type=F32)


def _prefix_max_rows(x):
    row = lax.broadcasted_iota(jnp.int32, x.shape, 0)
    s = 1
    while s < x.shape[0]:
        x = jnp.where(row >= s, jnp.maximum(x, pltpu.roll(x, s, 0)), x)
        s *= 2
    return x


def _state_spec(shape, layer):
    return pl.BlockSpec((None, 1) + shape, lambda i, j: (layer, i) + (0,) * len(shape))


def _state_out_spec(shape):
    return pl.BlockSpec((1,) + shape, lambda i, j: (i,) + (0,) * len(shape))


def _mlstm_body(q_ref, k_ref, v_ref, gc_ref, gr_ref, c0_ref, n0_ref, m0_ref,
                h_ref, c_ref, n_ref, m_ref):
    @pl.when(pl.program_id(1) == 0)
    def _():
        c_ref[...] = c0_ref[...]
        n_ref[...] = n0_ref[...]
        m_ref[...] = m0_ref[...]

    t = q_ref.shape[0]
    scale = C_HEAD_DIM ** -0.5
    lower, upper = _tri_masks(t)
    gc = gc_ref[...]
    gr = gr_ref[...]
    b_c = pltpu.roll(_cumsum_cols(lower, _log_sigmoid(gc)), GATE_COLS - FG_OFF, 1)
    b_r = _cumsum_rows(upper, _log_sigmoid(gr))[FG_OFF:FG_OFF + C_HEADS, :]
    c_c = gc - b_c
    c_r = gr[IG_OFF:IG_OFF + C_HEADS, :] - b_r
    m8 = m_ref[0]
    eye = lax.broadcasted_iota(jnp.int32, m8.shape, 0) == lax.broadcasted_iota(jnp.int32, m8.shape, 1)
    ms_row = jnp.sum(jnp.where(eye, m8, 0.0), axis=0, keepdims=True)
    big_m = jnp.maximum(ms_row, _prefix_max_rows(c_c))
    gw_all = jnp.exp(ms_row - big_m)
    floor_all = jnp.exp(-(b_c + big_m))
    m_last = big_m[t - 1:t, :]
    wj_all = jnp.exp(c_c - m_last)
    decay_all = jnp.exp(ms_row - m_last)
    m_new_all = b_c[t - 1:t, :] + m_last
    for h in range(C_HEADS):
        hs = slice(h * C_HEAD_DIM, (h + 1) * C_HEAD_DIM)
        col = lambda a: a[:, h:h + 1]
        qf = q_ref[:, hs]
        kf = k_ref[:, hs] * scale
        vb = v_ref[:, hs].astype(BF16)
        q2, k2 = _split_bf16(qf), _split_bf16(kf)
        w = _split_dot(_nt_dot, q2, k2) * jnp.exp(jnp.where(lower, c_r[h:h + 1, :] - col(big_m), -jnp.inf))
        gw = col(gw_all)
        cs = c_ref[0, h]
        ns = n_ref[0, h:h + 1, :]
        w_hi, w_lo = _split_bf16(w)
        num = _dot(w_hi, vb) + _dot(w_lo, vb) + gw * _split_dot(_dot, q2, _split_bf16(cs))
        nq = jnp.sum(w, axis=1, keepdims=True) + gw * jnp.sum(qf * ns, axis=1, keepdims=True)
        h_ref[:, hs] = num * (1.0 / jnp.maximum(jnp.abs(nq), col(floor_all)))
        decay = col(decay_all)
        kw = kf * col(wj_all)
        kw_hi, kw_lo = _split_bf16(kw)
        c_ref[0, h] = decay * cs + _tn_dot(kw_hi, vb) + _tn_dot(kw_lo, vb)
        n_ref[0, h:h + 1, :] = decay * ns + jnp.sum(kw, axis=0, keepdims=True)
        m_ref[0, h:h + 1, :] = jnp.broadcast_to(col(m_new_all), (1, C_HEAD_DIM))


def mlstm(proj, gates_c, gates_r, c0, n0, m0, layer, *, batch, seq, t):
    nb = seq // t
    row = lambda i, j: i * nb + j
    rows = lambda width, col: pl.BlockSpec((t, width), lambda i, j: (row(i, j), col))
    shapes = (c0.shape[2:], n0.shape[2:], m0.shape[2:])
    return pl.pallas_call(
        _mlstm_body,
        out_shape=(jax.ShapeDtypeStruct((batch * seq, C_WIDTH), F32),)
        + tuple(jax.ShapeDtypeStruct((batch,) + s, F32) for s in shapes),
        grid=(batch, nb),
        in_specs=[
            rows(C_WIDTH, ODD_QC), rows(C_WIDTH, ODD_KC), rows(C_WIDTH, ODD_VC), rows(GATE_COLS, 0),
            pl.BlockSpec((None, N_GATES, t), lambda i, j: (row(i, j), 0, 0)),
        ] + [_state_spec(s, layer) for s in shapes],
        out_specs=(rows(C_WIDTH, 0),) + tuple(_state_out_spec(s) for s in shapes),
        compiler_params=_params(("arbitrary", "arbitrary")),
        name="mlstm",
    )(proj, proj, proj, gates_c, gates_r, c0, n0, m0)


def _ssd_body(xd_ref, bc_ref, gc_ref, gr_ref, wx_ref, wbc_ref, bx_ref, bbc_ref,
              alr_ref, alc_ref, dskip_ref, hx0_ref, hbc0_ref, s0_ref,
              y_ref, s_ref, hxo_ref, hbco_ref, hx, hbc):
    @pl.when(pl.program_id(1) == 0)
    def _():
        s_ref[...] = s0_ref[...]
        hx[...] = hx0_ref[0]
        hbc[...] = hbc0_ref[0]

    t = xd_ref.shape[0]
    xr = xd_ref[...]
    br = bc_ref[...]
    xs = _silu(_causal_dwconv(xr, hx[...], wx_ref, bx_ref))
    bcs = _silu(_causal_dwconv(br, hbc[...], wbc_ref, bbc_ref))
    hx[...] = xr[t - SUBLANES:t]
    hbc[...] = br[t - SUBLANES:t]
    hxo_ref[0] = xr[t - SUBLANES:t]
    hbco_ref[0] = br[t - SUBLANES:t]

    lower, upper = _tri_masks(t)
    dt_c = _softplus(gc_ref[:, DT_OFF:DT_OFF + D_HEADS])
    dt_r = _softplus(gr_ref[DT_OFF:DT_OFF + D_HEADS, :])
    a_c = _cumsum_cols(lower, dt_c * -jnp.exp(alr_ref[...]))
    a_r = _cumsum_rows(upper, dt_r * -jnp.exp(alc_ref[...]))
    for g in range(D_GROUPS):
        bmg = bcs[:, g * D_STATE:(g + 1) * D_STATE].astype(BF16)
        cmg = bcs[:, (D_GROUPS + g) * D_STATE:(D_GROUPS + g + 1) * D_STATE].astype(BF16)
        cb = _nt_dot(cmg, bmg)
        for hh in range(D_HG):
            h = g * D_HG + hh
            hs = slice(h * D_HEAD_DIM, (h + 1) * D_HEAD_DIM)
            acb = jnp.broadcast_to(a_c[:, h:h + 1], (t, max(t, D_HEAD_DIM)))
            ace = acb[:, :D_HEAD_DIM]
            lw = jnp.exp(jnp.where(lower, acb[:, :t] - a_r[h:h + 1, :], -jnp.inf))
            xh = xs[:, hs]
            sh = s_ref[0, h]
            y = _dot((cb * (lw * dt_r[h:h + 1, :])).astype(BF16), xh.astype(BF16))
            y = y + jnp.exp(ace) * _nt_dot(cmg, sh.astype(BF16))
            y_ref[:, hs] = y + dskip_ref[:, hs] * xh
            at = a_c[t - 1:t, h:h + 1]
            wj = jnp.exp(at - ace) * dt_c[:, h:h + 1]
            s_ref[0, h] = jnp.exp(at) * sh + _tn_dot((xh * wj).astype(BF16), bmg)


def ssd(proj, gates_c, gates_r, wx, wbc, bx, bbc, alog_r, alog_c, dskip, hx0, hbc0, s0, layer, state_layer,
        *, batch, seq, t):
    nb = seq // t
    row = lambda i, j: i * nb + j
    rows = lambda width, col: pl.BlockSpec((t, width), lambda i, j: (row(i, j), col))
    param = lambda a: _layer_param(a.shape[1:], layer)
    shapes = (s0.shape[2:], hx0.shape[2:], hbc0.shape[2:])
    return pl.pallas_call(
        _ssd_body,
        out_shape=(jax.ShapeDtypeStruct((batch * seq, D_WIDTH), F32),)
        + tuple(jax.ShapeDtypeStruct((batch,) + s, F32) for s in shapes),
        grid=(batch, nb),
        in_specs=[
            rows(D_WIDTH, ODD_XD), rows(D_BC, ODD_BC_BLOCK),
            rows(GATE_COLS, 0), pl.BlockSpec((None, N_GATES, t), lambda i, j: (row(i, j), 0, 0)),
            param(wx), param(wbc), param(bx), param(bbc), param(alog_r), param(alog_c), param(dskip),
            _state_spec(shapes[1], state_layer), _state_spec(shapes[2], state_layer), _state_spec(shapes[0], state_layer),
        ],
        out_specs=(rows(D_WIDTH, 0),) + tuple(_state_out_spec(s) for s in shapes),
        scratch_shapes=[pltpu.VMEM((SUBLANES, D_WIDTH), F32), pltpu.VMEM((SUBLANES, D_BC), F32)],
        compiler_params=_params(("arbitrary", "arbitrary")),
        name="ssd",
    )(proj, proj, gates_c, gates_r, wx, wbc, bx, bbc, alog_r, alog_c, dskip, hx0, hbc0, s0)


def _pad_state_rows(buf):
    pad = jnp.zeros(buf.shape[:-2] + (SUBLANES - buf.shape[-2], buf.shape[-1]), F32)
    return jnp.concatenate([pad, buf], axis=-2)


def _row_gates(gates_c, t):
    m = gates_c.shape[0]
    return gates_c[:, :N_GATES].reshape(m // t, t, N_GATES).transpose(0, 2, 1)


def _even_col_map(j):
    n_gate, n_head = MIX_WIDTH // 512, (3 * A_WIDTH + 3 * B_WIDTH) // 512
    return jnp.where(j < n_gate, j + n_head + 1, jnp.where(j == n_gate, n_head, j - n_gate - 1))


ODD_C4 = 4 * C_WIDTH
ODD_XBC0 = ODD_C4 + 2 * C_HEADS
ODD_DT0 = ODD_XBC0 + D_XBC
ODD_QM0 = ODD_DT0 + D_HEADS
ODD_GATE0 = ODD_QM0 + M_WIDTH
ODD_BIG = MIX_WIDTH + M_WIDTH + ODD_C4 + D_XBC
PREP_COLS = 512


def _odd_src_col(c):
    n_gate, n_c4 = MIX_WIDTH // PREP_COLS, ODD_C4 // PREP_COLS
    return jnp.where(c < n_gate, ODD_GATE0 + PREP_COLS * c,
                     jnp.where(c == n_gate, ODD_QM0,
                               jnp.where(c <= n_gate + n_c4, PREP_COLS * (c - n_gate - 1),
                                         ODD_XBC0 + PREP_COLS * (c - n_gate - 1 - n_c4))))


def _prep_big_body(wt_ref, o_ref):
    o_ref[...] = wt_ref[0].T.astype(BF16)


def _prep_lo_body(wt_ref, o_ref):
    o_ref[...] = _split_bf16(wt_ref[...].T)[1]


def _prep_gates_body(a_ref, b_ref, hi_ref, lo_ref):
    pad = jnp.zeros((GATE_COLS - N_GATES, a_ref.shape[1]), F32)
    hi, lo = _split_bf16(jnp.concatenate([a_ref[...], b_ref[...], pad], axis=0).T)
    hi_ref[...] = hi
    lo_ref[...] = lo


def _odd_weights(w_in, b_in):
    layers, k, n = w_in.shape
    wt = jnp.swapaxes(w_in, 1, 2)
    out_block = pl.BlockSpec((None, k, PREP_COLS), lambda l, c: (l, 0, c))
    big_w = pl.pallas_call(
        _prep_big_body,
        out_shape=jax.ShapeDtypeStruct((layers, k, ODD_BIG), BF16),
        grid=(layers, ODD_BIG // PREP_COLS),
        in_specs=[pl.BlockSpec((pl.Element(1), pl.Element(PREP_COLS), pl.Element(k)),
                               lambda l, c: (l, pl.multiple_of(_odd_src_col(c), 2 * C_HEADS), 0))],
        out_specs=out_block,
        compiler_params=_params(("parallel", "parallel")),
        name="odd_weight_prep",
    )(wt)
    qk_lo = pl.pallas_call(
        _prep_lo_body,
        out_shape=jax.ShapeDtypeStruct((layers, k, 2 * C_WIDTH), BF16),
        grid=(layers, 2 * C_WIDTH // PREP_COLS),
        in_specs=[pl.BlockSpec((None, PREP_COLS, k), lambda l, c: (l, c, 0))],
        out_specs=out_block,
        compiler_params=_params(("parallel", "parallel")),
        name="odd_weight_prep_lo",
    )(wt)
    rows16 = lambda first: pl.BlockSpec((None, 2 * C_HEADS, k), lambda l: (l, first // (2 * C_HEADS), 0))
    gate_out = pl.BlockSpec((None, k, GATE_COLS), lambda l: (l, 0, 0))
    assert D_HEADS == 2 * C_HEADS and ODD_C4 % D_HEADS == 0 and ODD_DT0 % D_HEADS == 0
    small_hi, small_lo = pl.pallas_call(
        _prep_gates_body,
        out_shape=(jax.ShapeDtypeStruct((layers, k, GATE_COLS), BF16),) * 2,
        grid=(layers,),
        in_specs=[rows16(ODD_C4), rows16(ODD_DT0)],
        out_specs=(gate_out, gate_out),
        compiler_params=_params(("parallel",)),
        name="odd_weight_prep_gates",
    )(wt, wt)
    big = lambda a: jnp.concatenate(
        [a[..., ODD_GATE0:], a[..., ODD_QM0:ODD_GATE0], a[..., :ODD_C4], a[..., ODD_XBC0:ODD_DT0]], axis=-1)
    small = lambda a: jnp.concatenate(
        [a[..., ODD_C4:ODD_XBC0], a[..., ODD_DT0:ODD_QM0], jnp.zeros(a.shape[:-1] + (GATE_COLS - N_GATES,), a.dtype)],
        axis=-1)
    return (big_w, big(b_in)[:, None, :], qk_lo), (small_hi, small(b_in)[:, None, :], small_lo)


def _heads(a, batch, seq, col, last):
    part = a.reshape(batch, seq, a.shape[1])[:, seq - last:, col * A_WIDTH:(col + 1) * A_WIDTH]
    return part.reshape(batch, last, A_HEADS, A_HEAD_DIM)


def kernel(x_prompt, x_sample, mem_prompt, cache_a_k, cache_a_v, cache_mem_k, cache_mem_v, state_b_conv, state_c_C, state_c_n, state_c_m, state_d_ssm, state_d_conv, w_in_even, b_in_even, w_in_odd, b_in_odd, w_out, w_mem_kv, ln_g, ln_b, rel_bias_table, b_conv_w, b_conv_b, d_conv_w, d_conv_b, d_A_log, d_skip, d_norm_w):
    bp, sp, _ = x_prompt.shape
    bs, ss, _ = x_sample.shape
    lc = cache_a_k.shape[2]
    xp = x_prompt.reshape(bp * sp, D_MODEL)
    xs = x_sample.reshape(bs * ss, D_MODEL)
    xp16, xs16 = xp.astype(BF16), xs.astype(BF16)
    mem = mem_prompt.reshape(bp * MEM_LEN, D_MODEL).astype(BF16)

    we, be = w_in_even.astype(BF16), b_in_even[:, None, :]
    (wo_big, bo_big, wo_qk_lo), (wo_small, bo_small, wo_small_lo) = _odd_weights(w_in_odd, b_in_odd)
    qk_blocks = (ODD_QC * C_WIDTH // 512, (ODD_KC + 1) * C_WIDTH // 512)
    w_out_b = w_out.astype(BF16)
    w_mem_b = w_mem_kv.astype(BF16)
    zero_bias = jnp.zeros((1, 1, 2 * M_WIDTH), F32)
    ln_g3, ln_b3 = ln_g[:, None, :], ln_b[:, None, :]
    conv_b3 = b_conv_b[:, None, :]
    norm_w3 = d_norm_w[:, None, :]
    d_wx, d_wbc = d_conv_w[:, :, :D_WIDTH], d_conv_w[:, :, D_WIDTH:]
    d_bx, d_bbc = d_conv_b[:, None, :D_WIDTH], d_conv_b[:, None, D_WIDTH:]
    alog_r, alog_c = d_A_log[:, None, :], d_A_log[:, :, None]
    dskip = jnp.repeat(d_skip, D_HEAD_DIM, axis=-1)[:, None, :]

    s_halo_b = _pad_state_rows(state_b_conv)
    s_halo_d = _pad_state_rows(state_d_conv)
    s_hx0, s_hbc0 = s_halo_d[..., :D_WIDTH], s_halo_d[..., D_WIDTH:]
    s_m0 = jnp.broadcast_to(state_c_m[..., None], state_c_m.shape + (C_HEAD_DIM,))
    zeros = lambda *shape: jnp.zeros((1, bp) + shape, F32)
    p_halo_b, p_hx0, p_hbc0 = zeros(SUBLANES, B_WIDTH), zeros(SUBLANES, D_WIDTH), zeros(SUBLANES, D_BC)
    p_c0, p_n0, p_m0 = zeros(C_HEADS, C_HEAD_DIM, C_HEAD_DIM), zeros(C_HEADS, C_HEAD_DIM), zeros(C_HEADS, C_HEAD_DIM)
    p_s0 = zeros(D_HEADS, D_HEAD_DIM, D_STATE)

    outs = {k: [] for k in ("p_mk", "p_mv", "p_ak", "p_av", "p_bc", "p_cc", "p_cn", "p_cm", "p_ds", "p_dc",
                            "s_ak", "s_av", "s_bc", "s_cc", "s_cn", "s_cm", "s_ds", "s_dc")}
    tp = min(SCAN_BLOCK, sp)
    ts = min(SCAN_BLOCK, ss)
    for l in range(DEPTH):
        i = l // 2
        mkv = linear(mem, w_mem_b, zero_bias.repeat(DEPTH, 0), l)
        mkv5 = mkv.reshape(1, bp, MEM_LEN, 2 * M_HEADS, M_HEAD_DIM)
        mk5, mv5 = mkv5[:, :, :, :M_HEADS], mkv5[:, :, :, M_HEADS:]
        outs["p_mk"].append(mk5[0])
        outs["p_mv"].append(mv5[0])
        flat = lambda a: a.reshape(a.shape[0], a.shape[1], MEM_LEN * M_HEADS, M_HEAD_DIM)
        p_mem = (flat(mk5), flat(mv5), lambda b: (0, b, 0, 0), lambda b: (0, b, 0, 0))
        s_mem = (flat(cache_mem_k), flat(cache_mem_v), lambda b, l=l: (l, b, 0, 0), lambda b, l=l: (l, b, 0, 0))
        if l % 2 == 0:
            proj_p = linear(xp16, we, be, i, col_map=_even_col_map)
            proj_s = linear(xs16, we, be, i, col_map=_even_col_map)
            ya_p = band_attention_prompt(proj_p, rel_bias_band(rel_bias_table[i]), tq=A_WINDOW)
            bias_s = rel_bias(rel_bias_table[i], ss, lc + ss, lc)
            rows_of = lambda a: a.reshape(a.shape[0], a.shape[1], lc * A_HEADS, A_HEAD_DIM)
            ya_s = band_attention_sample(proj_s, rows_of(cache_a_k), rows_of(cache_a_v), i,
                                         bias_s[:, :, :lc], bias_s[:, :, lc:], t=ss)
            ym_p = mem_attention(proj_p, *p_mem, batch=bp, seq=sp, tl=min(sp, 512))
            ym_s = mem_attention(proj_s, *s_mem, batch=bs, seq=ss, tl=min(ss, 512))
            tail = (b_conv_w, conv_b3)
            xp_new, xp16_new, xplo_new, tail_p = even_tail(proj_p, ya_p, ym_p, xp, *tail, p_halo_b, 0, w_out_b, ln_g3, ln_b3,
                                                 i, l, seq=sp, tm=min(sp, 256))
            xs_new, xs16_new, xslo_new, tail_s = even_tail(proj_s, ya_s, ym_s, xs, *tail, s_halo_b, i, w_out_b, ln_g3, ln_b3,
                                                 i, l, seq=ss, tm=min(ss, 256))
            outs["p_ak"].append(_heads(proj_p, bp, sp, EVEN_KA, A_WINDOW))
            outs["p_av"].append(_heads(proj_p, bp, sp, EVEN_VA, A_WINDOW))
            outs["s_ak"].append(_heads(proj_s, bs, ss, EVEN_KA, ss))
            outs["s_av"].append(_heads(proj_s, bs, ss, EVEN_VA, ss))
            outs["p_bc"].append(tail_p[:, SUBLANES - (B_CONV - 1):])
            outs["s_bc"].append(tail_s[:, SUBLANES - (B_CONV - 1):])
        else:
            ssd_params = (d_wx, d_wbc, d_bx, d_bbc, alog_r, alog_c, dskip)
            res = []
            for x, x16, xlo, batch, seq, t, mem_args, states, sl in (
                    (xp, xp16, xplo, bp, sp, tp, p_mem, (p_c0, p_n0, p_m0, p_hx0, p_hbc0, p_s0), 0),
                    (xs, xs16, xslo, bs, ss, ts, s_mem, (state_c_C, state_c_n, s_m0, s_hx0, s_hbc0, state_d_ssm), i)):
                c0, n0, m0, hx0, hbc0, s0 = states
                proj = linear(x16, wo_big, bo_big, i,
                              fine=(xlo, wo_qk_lo, *qk_blocks) if l + 2 < DEPTH else None)
                gates_c = linear(x16, wo_small, bo_small, i, tn=GATE_COLS, fine=(xlo, wo_small_lo, 0, 1))
                gates_r = _row_gates(gates_c, t)
                hc, cc, cn, cm = mlstm(proj, gates_c, gates_r, c0, n0, m0, sl, batch=batch, seq=seq, t=t)
                yd, s_new, hx, hbc = ssd(proj, gates_c, gates_r, *ssd_params, hx0, hbc0, s0, i, sl,
                                         batch=batch, seq=seq, t=t)
                ym = mem_attention(proj, *mem_args, batch=batch, seq=seq, tl=min(seq, 512))
                x_new, x16_new = odd_tail(proj, hc, yd, ym, x, norm_w3, i, w_out_b, ln_g3, ln_b3, l, tm=256)
                dbuf = jnp.concatenate([hx, hbc], axis=-1)[:, SUBLANES - (D_CONV - 1):]
                res.append((x_new, x16_new, cc, cn, cm[:, :, 0], s_new, dbuf))
            (xp_new, xp16_new, *st_p), (xs_new, xs16_new, *st_s) = res
            for key, val in zip(("p_cc", "p_cn", "p_cm", "p_ds", "p_dc"), st_p):
                outs[key].append(val)
            for key, val in zip(("s_cc", "s_cn", "s_cm", "s_ds", "s_dc"), st_s):
                outs[key].append(val)
        xp, xs, xp16, xs16 = xp_new, xs_new, xp16_new, xs16_new
        if l % 2 == 0:
            xplo, xslo = xplo_new, xslo_new
    st = jnp.stack
    return (xp.reshape(bp, sp, D_MODEL), xs.reshape(bs, ss, D_MODEL),
            st(outs["p_mk"]), st(outs["p_mv"]), st(outs["p_ak"]), st(outs["p_av"]), st(outs["p_bc"]),
            st(outs["p_cc"]), st(outs["p_cn"]), st(outs["p_cm"]), st(outs["p_ds"]), st(outs["p_dc"]),
            st(outs["s_ak"]), st(outs["s_av"]), st(outs["s_bc"]),
            st(outs["s_cc"]), st(outs["s_cn"]), st(outs["s_cm"]), st(outs["s_ds"]), st(outs["s_dc"]))
```
